```python
import jax, jax.numpy as jnp
from jax import lax
import numpy as np

D_MODEL = 1024
BATCH = 2
SEQ = 16384
DEPTH = 1

CHUNK = 64
SGU_BLOCK = 128
SGU_WIDTH = D_MODEL
SGU_GROUPS = 8
SGU_GROUP_DIM = SGU_WIDTH // SGU_GROUPS
RWKV_HEAD = 64
RWKV_WIDTH = D_MODEL
RWKV_HEADS = RWKV_WIDTH // RWKV_HEAD
DECAY_LORA = max(32, int(round(1.8 * D_MODEL ** 0.5 / 32)) * 32)
ICLR_LORA = max(32, int(round(1.8 * D_MODEL ** 0.5 / 32)) * 32)
GATE_LORA = max(32, int(round(0.6 * D_MODEL ** 0.8 / 32)) * 32)
D_FF = int(round(8 * D_MODEL / 3 / 128)) * 128
CONV_WIDTH = 3
LN_EPS = 1e-5
GN_EPS = 64e-5
ALPHA = (2.0 * DEPTH) ** 0.25
BETA = (8.0 * DEPTH) ** -0.25
SHIFT_W = 3 * RWKV_WIDTH + DECAY_LORA + ICLR_LORA + GATE_LORA
IN_COLS = 2 * SGU_WIDTH + SHIFT_W + 2 * D_MODEL
IN_SPLITS = [SGU_WIDTH, 2 * SGU_WIDTH, 2 * SGU_WIDTH + SHIFT_W]
RWKV_SPLITS = [RWKV_WIDTH, 2 * RWKV_WIDTH, 3 * RWKV_WIDTH, 3 * RWKV_WIDTH + DECAY_LORA,
               3 * RWKV_WIDTH + DECAY_LORA + ICLR_LORA]

kernel_name = 'hybrid_sgu_rwkv7_convffn_deepnorm'


def layer_norm(x, g, b, eps=LN_EPS):
    xf = x.astype(jnp.float32)
    mu = jnp.mean(xf, -1, keepdims=True)
    var = jnp.mean(jnp.square(xf - mu), -1, keepdims=True)
    return ((xf - mu) * lax.rsqrt(var + eps) * g + b).astype(x.dtype)


def token_shift(h):
    return jnp.pad(h, ((0, 0), (1, 0), (0, 0)))[:, :-1]


def sgu_mixer(u, v, ln_g, ln_b, w_s, b_s):
    B_, S_, _ = v.shape
    nblk = S_ // SGU_BLOCK
    v = layer_norm(v, ln_g, ln_b).reshape(B_, nblk, SGU_BLOCK, SGU_GROUPS, SGU_GROUP_DIM)
    chunk_id = jnp.arange(SGU_BLOCK) // CHUNK
    mask = chunk_id[:, None] >= chunk_id[None, :]
    w = jnp.where(mask[None], w_s, 0.0)
    z = jnp.einsum('gij,bnjgc->bnigc', w, v) + b_s.T[None, None, :, :, None]
    return u * z.reshape(B_, S_, SGU_WIDTH)


def rwkv7_mixer(r, k, val, wd, ad, gd, w0, w2, a0, a2, g2, k_k, k_a, r_k, lnx_g, lnx_b):
    B_, S_, C = r.shape
    H, N = RWKV_HEADS, RWKV_HEAD
    f32 = jnp.float32
    wlog = -jax.nn.softplus(-(w0.astype(f32) + jnp.tanh(wd.astype(f32)) @ w2.astype(f32))) - 0.5
    decay = jnp.exp(-jnp.exp(wlog))
    a = jax.nn.sigmoid(a0 + ad @ a2)
    g = jax.nn.sigmoid(gd) @ g2
    heads = lambda t: t.astype(f32).reshape(B_, S_, H, N)
    kk = heads(k * k_k)
    kk = kk / jnp.maximum(jnp.sqrt(jnp.sum(kk * kk, -1, keepdims=True)), 1e-12)
    a_h = heads(a)
    k_h = heads(k * (1 + (a - 1) * k_a))
    r_h, v_h, w_h = heads(r), heads(val), heads(decay)

    def step(state, inp):
        r_t, w_t, k_t, v_t, a_t, b_t = inp
        sa = jnp.einsum('bhij,bhj->bhi', state, a_t)
        state = (state * w_t[:, :, None, :] + sa[..., None] * b_t[:, :, None, :]
                 + v_t[..., None] * k_t[:, :, None, :])
        return state, jnp.einsum('bhij,bhj->bhi', state, r_t)

    tm = lambda t: jnp.swapaxes(t, 0, 1)
    state0 = jnp.zeros((B_, H, N, N), f32)
    _, y = lax.scan(step, state0, (tm(r_h), tm(w_h), tm(k_h), tm(v_h), tm(-kk), tm(kk * a_h)))
    y = jnp.swapaxes(y, 0, 1)
    mu = jnp.mean(y, -1, keepdims=True)
    var = jnp.mean(jnp.square(y - mu), -1, keepdims=True)
    y = ((y - mu) * lax.rsqrt(var + GN_EPS)).reshape(B_, S_, C) * lnx_g + lnx_b
    bonus = jnp.sum(r_h * k_h * r_k.astype(f32), -1, keepdims=True) * v_h
    y = (y + bonus.reshape(B_, S_, C)) * g
    return y.astype(r.dtype)


def token_mixer(h, w_in, b_gate, mu_shift, sgu_ln_g, sgu_ln_b, w_s, b_s, w0, w2, a0, a2, g2,
                k_k, k_a, r_k, lnx_g, lnx_b, w_o):
    proj = h @ w_in
    u, v, rk, gates = jnp.split(proj, IN_SPLITS, axis=-1)
    y_a = sgu_mixer(jax.nn.gelu(u, approximate=False), jax.nn.gelu(v, approximate=False),
                    sgu_ln_g, sgu_ln_b, w_s, b_s)
    rk = rk + (token_shift(rk) - rk) * mu_shift
    r, k, val, wd, ad, gd = jnp.split(rk, RWKV_SPLITS, axis=-1)
    y_b = rwkv7_mixer(r, k, val, wd, ad, gd, w0, w2, a0, a2, g2, k_k, k_a, r_k, lnx_g, lnx_b)
    gate_a, gate_b = jnp.split(jax.nn.sigmoid(gates + b_gate), 2, axis=-1)
    return (gate_a * y_a + gate_b * y_b) @ w_o


def conv_ffn(h, w_up, conv_w, conv_b, w_down):
    gate, val = jnp.split(h @ w_up, 2, axis=-1)
    S_ = h.shape[1]
    pad = jnp.pad(gate, ((0, 0), (CONV_WIDTH - 1, 0), (0, 0)))
    conv = sum((pad[:, i:i + S_] * conv_w[i] for i in range(CONV_WIDTH)), conv_b)
    return (jax.nn.gelu(conv, approximate=False) * val) @ w_down


def setup_inputs(seed: int = 0) -> dict:
    key = jax.random.key(seed)
    ks = jax.random.split(key, 32)
    L, D, C = DEPTH, D_MODEL, RWKV_WIDTH
    nrm = lambda k, shape, s: jax.random.normal(k, shape, jnp.float32) * s
    gain = lambda k, shape: 1.0 + nrm(k, shape, 0.02)
    return {
        'x': nrm(ks[0], (BATCH, SEQ, D), 1.0),
        'ln_in_g': gain(ks[1], (D,)),
        'ln_in_b': nrm(ks[2], (D,), 0.02),
        'w_in': nrm(ks[3], (L, D, IN_COLS), D ** -0.5),
        'b_gate': nrm(ks[4], (L, 2 * D), 0.1),
        'mu_shift': jax.random.uniform(ks[5], (L, SHIFT_W), jnp.float32),
        'sgu_ln_g': gain(ks[6], (L, SGU_WIDTH)),
        'sgu_ln_b': nrm(ks[7], (L, SGU_WIDTH), 0.02),
        'w_s': nrm(ks[8], (L, SGU_GROUPS, SGU_BLOCK, SGU_BLOCK), SGU_BLOCK ** -0.5),
        'b_s': gain(ks[9], (L, SGU_GROUPS, SGU_BLOCK)),
        'w0': jax.random.uniform(ks[10], (L, C), jnp.float32, -6.0, -1.0),
        'w2': nrm(ks[11], (L, DECAY_LORA, C), 0.5 * DECAY_LORA ** -0.5),
        'a0': nrm(ks[12], (L, C), 0.1),
        'a2': nrm(ks[13], (L, ICLR_LORA, C), 0.5 * ICLR_LORA ** -0.5),
        'g2': nrm(ks[14], (L, GATE_LORA, C), GATE_LORA ** -0.5),
        'k_k': 0.85 + nrm(ks[15], (L, C), 0.02),
        'k_a': gain(ks[16], (L, C)),
        'r_k': nrm(ks[17], (L, RWKV_HEADS, RWKV_HEAD), 0.1),
        'lnx_g': gain(ks[18], (L, C)),
        'lnx_b': nrm(ks[19], (L, C), 0.02),
        'w_o': nrm(ks[20], (L, D, D), BETA * D ** -0.5),
        'ln1_g': gain(ks[21], (L, D)),
        'ln1_b': nrm(ks[22], (L, D), 0.02),
        'w_up': nrm(ks[23], (L, D, 2 * D_FF), D ** -0.5),
        'conv_w': nrm(ks[24], (L, CONV_WIDTH, D_FF), CONV_WIDTH ** -0.5),
        'conv_b': nrm(ks[25], (L, D_FF), 0.02),
        'w_down': nrm(ks[26], (L, D_FF, D), BETA * D_FF ** -0.5),
        'ln2_g': gain(ks[27], (L, D)),
        'ln2_b': nrm(ks[28], (L, D), 0.02),
    }


def reference(x, ln_in_g, ln_in_b, w_in, b_gate, mu_shift, sgu_ln_g, sgu_ln_b, w_s, b_s,
              w0, w2, a0, a2, g2, k_k, k_a, r_k, lnx_g, lnx_b, w_o, ln1_g, ln1_b,
              w_up, conv_w, conv_b, w_down, ln2_g, ln2_b):
    x = layer_norm(x, ln_in_g, ln_in_b)
    for l in range(DEPTH):
        mix = token_mixer(x, w_in[l], b_gate[l], mu_shift[l], sgu_ln_g[l], sgu_ln_b[l], w_s[l], b_s[l],
                          w0[l], w2[l], a0[l], a2[l], g2[l], k_k[l], k_a[l], r_k[l], lnx_g[l], lnx_b[l],
                          w_o[l])
        x = layer_norm(ALPHA * x + mix, ln1_g[l], ln1_b[l])
        x = layer_norm(ALPHA * x + conv_ffn(x, w_up[l], conv_w[l], conv_b[l], w_down[l]), ln2_g[l], ln2_b[l])
    return x
```

```python
import functools
import math

import jax
import jax.numpy as jnp
from jax import lax
from jax.experimental import pallas as pl
from jax.experimental.pallas import tpu as pltpu

D_MODEL = 1024
SGU_BLOCK = 128
SGU_GROUPS = 8
HEAD = 64
HEADS = D_MODEL // HEAD
DECAY_LORA = 64
ICLR_LORA = 64
GATE_LORA = 160
GATE_LORA_PAD = 256
D_FF = 2688
CHUNK = 64
LN_EPS = 1e-5
GN_EPS = 64e-5
ALPHA = 2.0 ** 0.25
RK_COLS = 3 * D_MODEL + 128 + GATE_LORA_PAD

PRE_ROWS = 256
POST_ROWS = 256
VMEM_LIMIT = 56 * 1024 * 1024

F32 = jnp.float32
BF16 = jnp.bfloat16
HI = lax.Precision.HIGHEST


def _ln(x, g, b, eps):
    mu = jnp.mean(x, -1, keepdims=True)
    xc = x - mu
    var = jnp.mean(xc * xc, -1, keepdims=True)
    return xc * lax.rsqrt(var + eps) * g + b


def _gelu(x):
    return 0.5 * x * (1.0 + lax.erf(x * (1.0 / math.sqrt(2.0))))


def _sigmoid(x):
    return 1.0 / (1.0 + jnp.exp(-x))


def _bdot(a, b):
    return jnp.dot(a.astype(BF16), b.astype(BF16), preferred_element_type=F32)


def _head_sum(q, ones_blk):
    q_hi = q.astype(BF16)
    q_lo = (q - q_hi.astype(F32)).astype(BF16)
    cols = []
    for g in range(q.shape[1] // 128):
        sl = slice(g * 128, (g + 1) * 128)
        cols.append(jnp.dot(q_hi[:, sl], ones_blk, preferred_element_type=F32)
                    + jnp.dot(q_lo[:, sl], ones_blk, preferred_element_type=F32))
    return jnp.concatenate(cols, axis=1)


def _shift_rows(x, carry, k):
    rolled = pltpu.roll(x, k, 0)
    prev = pltpu.roll(carry, k, 0)
    row = lax.broadcasted_iota(jnp.int32, x.shape, 0)
    head = jnp.concatenate([prev, rolled[8:]], axis=0)
    return jnp.where(row < k, head, rolled)


def _pre_kernel(tiles_per_seq,
                x_ref, lng_ref, lnb_ref, wuv_ref, wrk_ref, wg_ref, bg_ref, mu_ref,
                sg_ref, sb_ref, ws_ref, bst_ref, w0_ref, w2_ref, a0_ref, a2_ref, g2_ref,
                kk_ref, ka_ref, ones_ref,
                r_out, lw_out, k_out, v_out, kk_out, a_out, ca_out, c1_out,
                carry_ref):
    i = pl.program_id(0)

    @pl.when(i % tiles_per_seq == 0)
    def _():
        carry_ref[...] = jnp.zeros_like(carry_ref)

    h = _ln(x_ref[...], lng_ref[...], lnb_ref[...], LN_EPS)
    hb = h.astype(BF16)

    uv = jnp.dot(hb, wuv_ref[...], preferred_element_type=F32)
    u = _gelu(uv[:, :D_MODEL])
    vn = _ln(_gelu(uv[:, D_MODEL:]), sg_ref[...], sb_ref[...], LN_EPS).astype(BF16)
    rr = lax.broadcasted_iota(jnp.int32, (SGU_BLOCK, SGU_BLOCK), 0) // CHUNK
    cc = lax.broadcasted_iota(jnp.int32, (SGU_BLOCK, SGU_BLOCK), 1) // CHUNK
    blk_mask = rr >= cc
    bst = bst_ref[...]
    z_rows = []
    for nb in range(PRE_ROWS // SGU_BLOCK):
        z_cols = []
        for g in range(SGU_GROUPS):
            wm = jnp.where(blk_mask, ws_ref[g], 0.0).astype(BF16)
            vb = vn[nb * SGU_BLOCK:(nb + 1) * SGU_BLOCK, g * 128:(g + 1) * 128]
            z_cols.append(jnp.dot(wm, vb, preferred_element_type=F32) + bst[:, g:g + 1])
        z_rows.append(jnp.concatenate(z_cols, axis=1))
    ya = u * jnp.concatenate(z_rows, axis=0)

    gates = _sigmoid(jnp.dot(hb, wg_ref[...], preferred_element_type=F32) + bg_ref[...])
    ca_out[...] = gates[:, :D_MODEL] * ya
    gb = gates[:, D_MODEL:]

    rk = jnp.dot(hb, wrk_ref[...], preferred_element_type=F32)
    rk_prev = _shift_rows(rk, carry_ref[...], 1)
    carry_ref[...] = rk[PRE_ROWS - 8:, :]
    rk = rk + (rk_prev - rk) * mu_ref[...]
    r = rk[:, :D_MODEL]
    k = rk[:, D_MODEL:2 * D_MODEL]
    val = rk[:, 2 * D_MODEL:3 * D_MODEL]
    wdad = rk[:, 3 * D_MODEL:3 * D_MODEL + 128]
    gd = rk[:, 3 * D_MODEL + 128:]

    zw = w0_ref[...] + _bdot(jnp.tanh(wdad), w2_ref[...])
    lw_out[...] = (-math.exp(-0.5)) * _sigmoid(zw)
    a_lr = _sigmoid(a0_ref[...] + _bdot(wdad, a2_ref[...]))
    g = _bdot(_sigmoid(gd), g2_ref[...])
    c1_out[...] = gb * g

    kk = k * kk_ref[...]
    nrm = jnp.sqrt(_head_sum(kk * kk, ones_ref[...]))
    kk_out[...] = kk / jnp.maximum(nrm, 1e-12)
    k_out[...] = k * (1.0 + (a_lr - 1.0) * ka_ref[...])
    r_out[...] = r
    v_out[...] = val
    a_out[...] = a_lr


def _hdot(a, b):
    return jnp.dot(a, b, precision=HI, preferred_element_type=F32)


def _hdot_nt(a, b):
    return lax.dot_general(a, b, (((1,), (1,)), ((), ())), precision=HI, preferred_element_type=F32)


def _hdot_tn(a, b):
    return lax.dot_general(a, b, (((0,), (0,)), ((), ())), precision=HI, preferred_element_type=F32)


def _scan_kernel(r_ref, lw_ref, k_ref, v_ref, kk_ref, a_ref, rk_ref, gng_ref, gnb_ref,
                 y_out, state_ref):
    c = pl.program_id(1)

    @pl.when(c == 0)
    def _():
        state_ref[...] = jnp.zeros_like(state_ref)

    L = CHUNK
    row = lax.broadcasted_iota(jnp.int32, (L, L), 0)
    col = lax.broadcasted_iota(jnp.int32, (L, L), 1)
    tri_incl = (row >= col).astype(F32)
    low_strict = row > col
    low_incl = row >= col
    eye = row == col

    lw = lw_ref[...]
    cs = _hdot(tri_incl, lw)
    e_pos = jnp.exp(cs)
    e_neg = jnp.exp(-cs)
    p_end = e_pos[L - 1:L, :]
    r = r_ref[...]
    kh = k_ref[...]
    v = v_ref[...]
    kk = kk_ref[...]
    a_t = -kk * jnp.exp(cs - lw)
    r_t = r * e_pos
    b_t = kk * a_ref[...] * e_neg
    k_t = kh * e_neg
    b_h = b_t * p_end
    k_h = k_t * p_end
    bonus_w = r * kh * rk_ref[...]

    outs = []
    for hd in range(HEADS):
        sl = slice(hd * HEAD, (hd + 1) * HEAD)
        at, rt, bt, kt, bh, kth, vh = a_t[:, sl], r_t[:, sl], b_t[:, sl], k_t[:, sl], b_h[:, sl], k_h[:, sl], v[:, sl]
        a_ab = jnp.where(low_strict, _hdot_nt(at, bt), 0.0)
        a_ak = jnp.where(low_strict, _hdot_nt(at, kt), 0.0)
        a_rb = jnp.where(low_incl, _hdot_nt(rt, bt), 0.0)
        a_rk = jnp.where(low_incl, _hdot_nt(rt, kt), 0.0)
        w = at
        u0 = _hdot(a_ak, vh)
        npow = a_ab
        for lev in range(6):
            w = w + _hdot(npow, w)
            u0 = u0 + _hdot(npow, u0)
            if lev < 5:
                npow = _hdot(npow, npow)
        m = jnp.where(eye, p_end[:, sl], 0.0) + _hdot_tn(bh, w)
        cst = _hdot_tn(bh, u0) + _hdot_tn(kth, vh)
        q = rt + _hdot(a_rb, w)
        y0 = _hdot(a_rb, u0) + _hdot(a_rk, vh)
        st = state_ref[hd]
        y = _hdot(q, st) + y0
        state_ref[hd] = _hdot(m, st) + cst
        mu = jnp.mean(y, -1, keepdims=True)
        yc = y - mu
        var = jnp.mean(yc * yc, -1, keepdims=True)
        bonus = jnp.sum(bonus_w[:, sl], -1, keepdims=True) * vh
        outs.append(yc * lax.rsqrt(var + GN_EPS) * gng_ref[:, sl] + gnb_ref[:, sl] + bonus)
    y_out[...] = jnp.concatenate(outs, axis=1)


def _post_kernel(tiles_per_seq,
                 x_ref, lng_ref, lnb_ref, ca_ref, c1_ref, yb_ref, wo_ref, l1g_ref, l1b_ref,
                 wup_ref, cw_ref, cb_ref, wdn_ref, l2g_ref, l2b_ref,
                 out_ref, carry_ref):
    i = pl.program_id(0)

    @pl.when(i % tiles_per_seq == 0)
    def _():
        carry_ref[...] = jnp.zeros_like(carry_ref)

    h = _ln(x_ref[...], lng_ref[...], lnb_ref[...], LN_EPS)
    mix = ca_ref[...] + c1_ref[...] * yb_ref[...]
    x1 = _ln(ALPHA * h + _bdot(mix, wo_ref[...]), l1g_ref[...], l1b_ref[...], LN_EPS)
    up = _bdot(x1, wup_ref[...])
    gate = up[:, :D_FF]
    val = up[:, D_FF:]
    carry = carry_ref[...]
    g1 = _shift_rows(gate, carry, 1)
    g2 = _shift_rows(gate, carry, 2)
    carry_ref[...] = gate[POST_ROWS - 8:, :]
    cw = cw_ref[...]
    conv = cb_ref[...] + g2 * cw[0:1, :] + g1 * cw[1:2, :] + gate * cw[2:3, :]
    ffn = _bdot(_gelu(conv) * val, wdn_ref[...])
    out_ref[...] = _ln(ALPHA * x1 + ffn, l2g_ref[...], l2b_ref[...], LN_EPS)


def _const_spec(shape):
    nd = len(shape)
    return pl.BlockSpec(shape, lambda *_: (0,) * nd, pipeline_mode=pl.Buffered(1))


def kernel(x, ln_in_g, ln_in_b, w_in, b_gate, mu_shift, sgu_ln_g, sgu_ln_b, w_s, b_s, w0, w2, a0, a2, g2, k_k, k_a, r_k, lnx_g, lnx_b, w_o, ln1_g, ln1_b, w_up, conv_w, conv_b, w_down, ln2_g, ln2_b):
    B, S, D = x.shape
    assert D == D_MODEL and w_in.shape[0] == 1
    T = B * S
    row = lambda p: p.reshape(1, -1).astype(F32)
    xf = x.reshape(T, D)

    wi = w_in[0]
    c_rk = 2 * D
    c_lora = c_rk + 3 * D
    c_gd = c_lora + DECAY_LORA + ICLR_LORA
    c_gates = c_gd + GATE_LORA
    w_uv = wi[:, :c_rk].astype(BF16)
    w_rk = jnp.concatenate(
        [wi[:, c_rk:c_gates], jnp.zeros((D, GATE_LORA_PAD - GATE_LORA), F32)], axis=1).astype(BF16)
    w_g = wi[:, c_gates:].astype(BF16)
    mu = jnp.concatenate([mu_shift[0], jnp.zeros((GATE_LORA_PAD - GATE_LORA,), F32)]).reshape(1, -1)
    w2p = jnp.concatenate([w2[0], jnp.zeros((ICLR_LORA, D), F32)], axis=0).astype(BF16)
    a2p = jnp.concatenate([jnp.zeros((DECAY_LORA, D), F32), a2[0]], axis=0).astype(BF16)
    g2p = jnp.concatenate([g2[0], jnp.zeros((GATE_LORA_PAD - GATE_LORA, D), F32)], axis=0).astype(BF16)
    lane = jnp.arange(128) // HEAD
    ones_blk = (lane[:, None] == lane[None, :]).astype(BF16)

    tiles_per_seq = S // PRE_ROWS
    tok = pl.BlockSpec((PRE_ROWS, D), lambda i: (i, 0))
    pre_in = [xf, row(ln_in_g), row(ln_in_b), w_uv, w_rk, w_g, row(b_gate), mu,
              row(sgu_ln_g), row(sgu_ln_b), w_s[0], b_s[0].T, row(w0), w2p, row(a0), a2p, g2p,
              row(k_k), row(k_a), ones_blk]
    pre_specs = [tok] + [_const_spec(a.shape) for a in pre_in[1:]]
    tok_shape = jax.ShapeDtypeStruct((T, D), F32)
    r_t, lw_t, k_t, v_t, kk_t, a_t, ca_t, c1_t = pl.pallas_call(
        functools.partial(_pre_kernel, tiles_per_seq),
        grid=(T // PRE_ROWS,),
        in_specs=pre_specs,
        out_specs=[tok] * 8,
        out_shape=[tok_shape] * 8,
        scratch_shapes=[pltpu.VMEM((8, RK_COLS), F32)],
        compiler_params=pltpu.CompilerParams(dimension_semantics=("arbitrary",),
                                             vmem_limit_bytes=VMEM_LIMIT),
        name="pre",
    )(*pre_in)

    n_chunks = S // CHUNK
    ctok = pl.BlockSpec((CHUNK, D), lambda b, c: (b * n_chunks + c, 0))
    prm = pl.BlockSpec((1, D), lambda b, c: (0, 0))
    yb_t = pl.pallas_call(
        _scan_kernel,
        grid=(B, n_chunks),
        in_specs=[ctok] * 6 + [prm] * 3,
        out_specs=ctok,
        out_shape=tok_shape,
        scratch_shapes=[pltpu.VMEM((HEADS, HEAD, HEAD), F32)],
        compiler_params=pltpu.CompilerParams(dimension_semantics=("arbitrary", "arbitrary"),
                                             vmem_limit_bytes=VMEM_LIMIT),
        name="scan",
    )(r_t, lw_t, k_t, v_t, kk_t, a_t, row(r_k), row(lnx_g), row(lnx_b))

    ptok = pl.BlockSpec((POST_ROWS, D), lambda i: (i, 0))
    post_in = [xf, row(ln_in_g), row(ln_in_b), ca_t, c1_t, yb_t, w_o[0].astype(BF16),
               row(ln1_g), row(ln1_b), w_up[0].astype(BF16), conv_w[0], row(conv_b),
               w_down[0].astype(BF16), row(ln2_g), row(ln2_b)]
    post_specs = [ptok, _const_spec((1, D)), _const_spec((1, D)), ptok, ptok, ptok] + \
                 [_const_spec(a.shape) for a in post_in[6:]]
    out = pl.pallas_call(
        functools.partial(_post_kernel, S // POST_ROWS),
        grid=(T // POST_ROWS,),
        in_specs=post_specs,
        out_specs=ptok,
        out_shape=tok_shape,
        scratch_shapes=[pltpu.VMEM((8, D_FF), F32)],
        compiler_params=pltpu.CompilerParams(dimension_semantics=("arbitrary",),
                                             vmem_limit_bytes=VMEM_LIMIT),
        name="post",
    )(*post_in)
    return out.reshape(B, S, D)
```

```python
import functools
import math

import jax
import jax.numpy as jnp
from jax import lax
from jax.experimental import pallas as pl
from jax.experimental.pallas import tpu as pltpu

D_MODEL = 1024
SGU_BLOCK = 128
SGU_GROUPS = 8
HEAD = 64
HEADS = D_MODEL // HEAD
DECAY_LORA = 64
ICLR_LORA = 64
GATE_LORA = 160
GATE_LORA_PAD = 256
D_FF = 2688
CHUNK = 64
LN_EPS = 1e-5
GN_EPS = 64e-5
ALPHA = 2.0 ** 0.25
RK_COLS = 3 * D_MODEL + 128 + GATE_LORA_PAD

PRE_ROWS = 256
POST_ROWS = 256
VMEM_LIMIT = 56 * 1024 * 1024

F32 = jnp.float32
BF16 = jnp.bfloat16


def _ln(x, g, b, eps):
    mu = jnp.mean(x, -1, keepdims=True)
    xc = x - mu
    var = jnp.mean(xc * xc, -1, keepdims=True)
    return xc * lax.rsqrt(var + eps) * g + b


def _gelu(x):
    return 0.5 * x * (1.0 + lax.erf(x * (1.0 / math.sqrt(2.0))))


def _sigmoid(x):
    return 1.0 / (1.0 + jnp.exp(-x))


def _bdot(a, b):
    return jnp.dot(a.astype(BF16), b.astype(BF16), preferred_element_type=F32)


def _head_sum(q, ones_blk):
    q_hi = q.astype(BF16)
    q_lo = (q - q_hi.astype(F32)).astype(BF16)
    cols = []
    for g in range(q.shape[1] // 128):
        sl = slice(g * 128, (g + 1) * 128)
        cols.append(jnp.dot(q_hi[:, sl], ones_blk, preferred_element_type=F32)
                    + jnp.dot(q_lo[:, sl], ones_blk, preferred_element_type=F32))
    return jnp.concatenate(cols, axis=1)


def _shift_rows(x, carry, k):
    rolled = pltpu.roll(x, k, 0)
    prev = pltpu.roll(carry, k, 0)
    row = lax.broadcasted_iota(jnp.int32, x.shape, 0)
    head = jnp.concatenate([prev, rolled[8:]], axis=0)
    return jnp.where(row < k, head, rolled)


def _pre_kernel(tiles_per_seq,
                x_ref, lng_ref, lnb_ref, wuv_ref, wrk_ref, wg_ref, bg_ref, mu_ref,
                sg_ref, sb_ref, ws_ref, bst_ref, w0_ref, w2_ref, a0_ref, a2_ref, g2_ref,
                kk_ref, ka_ref, ones_ref,
                r_out, lw_out, k_out, v_out, kk_out, a_out, ca_out, c1_out,
                carry_ref):
    i = pl.program_id(0)

    @pl.when(i % tiles_per_seq == 0)
    def _():
        carry_ref[...] = jnp.zeros_like(carry_ref)

    h = _ln(x_ref[...], lng_ref[...], lnb_ref[...], LN_EPS)
    hb = h.astype(BF16)

    uv = jnp.dot(hb, wuv_ref[...], preferred_element_type=F32)
    u = _gelu(uv[:, :D_MODEL])
    vn = _ln(_gelu(uv[:, D_MODEL:]), sg_ref[...], sb_ref[...], LN_EPS).astype(BF16)
    rr = lax.broadcasted_iota(jnp.int32, (SGU_BLOCK, SGU_BLOCK), 0) // CHUNK
    cc = lax.broadcasted_iota(jnp.int32, (SGU_BLOCK, SGU_BLOCK), 1) // CHUNK
    blk_mask = rr >= cc
    bst = bst_ref[...]
    z_rows = []
    for nb in range(PRE_ROWS // SGU_BLOCK):
        z_cols = []
        for g in range(SGU_GROUPS):
            wm = jnp.where(blk_mask, ws_ref[g], 0.0).astype(BF16)
            vb = vn[nb * SGU_BLOCK:(nb + 1) * SGU_BLOCK, g * 128:(g + 1) * 128]
            z_cols.append(jnp.dot(wm, vb, preferred_element_type=F32) + bst[:, g:g + 1])
        z_rows.append(jnp.concatenate(z_cols, axis=1))
    ya = u * jnp.concatenate(z_rows, axis=0)

    gates = _sigmoid(jnp.dot(hb, wg_ref[...], preferred_element_type=F32) + bg_ref[...])
    ca_out[...] = gates[:, :D_MODEL] * ya
    gb = gates[:, D_MODEL:]

    rk = jnp.dot(hb, wrk_ref[...], preferred_element_type=F32)
    rk_prev = _shift_rows(rk, carry_ref[...], 1)
    carry_ref[...] = rk[PRE_ROWS - 8:, :]
    rk = rk + (rk_prev - rk) * mu_ref[...]
    r = rk[:, :D_MODEL]
    k = rk[:, D_MODEL:2 * D_MODEL]
    val = rk[:, 2 * D_MODEL:3 * D_MODEL]
    wdad = rk[:, 3 * D_MODEL:3 * D_MODEL + 128]
    gd = rk[:, 3 * D_MODEL + 128:]

    zw = w0_ref[...] + _bdot(jnp.tanh(wdad), w2_ref[...])
    lw_out[...] = (-math.exp(-0.5)) * _sigmoid(zw)
    a_lr = _sigmoid(a0_ref[...] + _bdot(wdad, a2_ref[...]))
    g = _bdot(_sigmoid(gd), g2_ref[...])
    c1_out[...] = gb * g

    kk = k * kk_ref[...]
    nrm = jnp.sqrt(_head_sum(kk * kk, ones_ref[...]))
    kk_out[...] = kk / jnp.maximum(nrm, 1e-12)
    k_out[...] = k * (1.0 + (a_lr - 1.0) * ka_ref[...])
    r_out[...] = r
    v_out[...] = val
    a_out[...] = a_lr


def _dot_nt(a, b):
    return lax.dot_general(a, b, (((1,), (1,)), ((), ())), preferred_element_type=F32)


def _split(x):
    hi = x.astype(BF16)
    return hi, (x - hi.astype(F32)).astype(BF16)


def _pair_sum(q, ones2):
    hi, lo = _split(q)
    return jnp.dot(jnp.concatenate([hi, lo], axis=1), ones2, preferred_element_type=F32)


def _scan_kernel(r_ref, lw_ref, k_ref, v_ref, kk_ref, a_ref, rk_ref, gng_ref, gnb_ref,
                 tri2_ref, ones2_ref, y_out, state_ref):
    c = pl.program_id(1)

    @pl.when(c == 0)
    def _():
        state_ref[...] = jnp.zeros_like(state_ref)

    L = CHUNK
    P = 2 * HEAD
    lw = lw_ref[...]
    lw_hi, lw_lo = _split(lw)
    cs = jnp.dot(tri2_ref[...], jnp.concatenate([lw_hi, lw_lo], axis=0), preferred_element_type=F32)
    e_pos = jnp.exp(cs)
    e_neg = jnp.exp(-cs)
    p_end = e_pos[L - 1:L, :]
    r = r_ref[...]
    kh = k_ref[...]
    v = v_ref[...]
    kk = kk_ref[...]
    a_t = -kk * jnp.exp(cs - lw)
    r_t = r * e_pos
    b_t = kk * a_ref[...] * e_neg
    k_t = kh * e_neg
    b_h = b_t * p_end
    k_h = k_t * p_end
    bonus_w = r * kh * rk_ref[...]

    t_row = lax.broadcasted_iota(jnp.int32, (L, P), 0)
    lane = lax.broadcasted_iota(jnp.int32, (L, P), 1)
    first = lane < HEAD
    s_col = jnp.where(first, lane, lane - HEAD)
    strict = s_col < t_row
    incl = s_col <= t_row
    row2 = lax.broadcasted_iota(jnp.int32, (P, P), 0)
    lane2 = lax.broadcasted_iota(jnp.int32, (P, P), 1)
    same_head = (row2 < HEAD) == (lane2 < HEAD)
    eye2 = row2 == lane2
    zero = jnp.zeros((L, P), F32)
    ones2 = ones2_ref[...]

    pairs = range(HEADS // 2)
    sls = [slice(p * P, (p + 1) * P) for p in pairs]
    at1 = [jnp.where(first, a_t[:, s], 0.0) for s in sls]
    at2 = [jnp.where(first, 0.0, a_t[:, s]) for s in sls]
    rt1 = [jnp.where(first, r_t[:, s], 0.0) for s in sls]
    rt2 = [jnp.where(first, 0.0, r_t[:, s]) for s in sls]
    bt_b = [b_t[:, s].astype(BF16) for s in sls]
    kt_b = [k_t[:, s].astype(BF16) for s in sls]
    ga = [_dot_nt(jnp.concatenate([rt1[p], rt2[p], at1[p]], axis=0).astype(BF16),
                  jnp.concatenate([bt_b[p], kt_b[p]], axis=0)) for p in pairs]
    gb = [_dot_nt(at2[p].astype(BF16), jnp.concatenate([kt_b[p], bt_b[p]], axis=0)) for p in pairs]
    g_a1 = [jnp.where(strict, ga[p][2 * L:], 0.0) for p in pairs]
    g_a2 = [jnp.where(strict, gb[p], 0.0) for p in pairs]
    npow = [jnp.concatenate([jnp.where(first, g_a1[p], 0.0), jnp.where(first, 0.0, g_a2[p])], axis=0)
            for p in pairs]
    ak2 = [jnp.concatenate([jnp.where(first, 0.0, g_a1[p]), jnp.where(first, g_a2[p], 0.0)], axis=0)
           for p in pairs]
    zu = [jnp.dot(ak2[p].astype(BF16),
                  jnp.concatenate([jnp.where(first, 0.0, v[:, sls[p]]), jnp.where(first, v[:, sls[p]], 0.0)],
                                  axis=0).astype(BF16), preferred_element_type=F32) for p in pairs]
    z = [jnp.concatenate([jnp.concatenate([at1[p], at2[p]], axis=0), zu[p]], axis=1) for p in pairs]
    for lev in range(6):
        if lev < 5:
            o = [jnp.dot(npow[p].astype(BF16), jnp.concatenate([npow[p], z[p]], axis=1).astype(BF16),
                         preferred_element_type=F32) for p in pairs]
            npow = [o[p][:, :P] for p in pairs]
            z = [z[p] + o[p][:, P:] for p in pairs]
        else:
            z = [z[p] + jnp.dot(npow[p].astype(BF16), z[p].astype(BF16), preferred_element_type=F32)
                 for p in pairs]
    o = []
    for p in pairs:
        wn = z[p][:L, :P] + z[p][L:, :P]
        u0n = z[p][:L, P:] + z[p][L:, P:]
        rhs = jnp.concatenate([jnp.concatenate([wn, u0n], axis=1),
                               jnp.concatenate([zero, v[:, sls[p]]], axis=1)], axis=0).astype(BF16)
        t_all = jnp.concatenate([b_h[:, sls[p]], k_h[:, sls[p]]], axis=0).T
        lhs = jnp.concatenate([t_all, jnp.where(incl, ga[p][:L], 0.0), jnp.where(incl, ga[p][L:2 * L], 0.0)],
                              axis=0).astype(BF16)
        o.append(jnp.dot(lhs, rhs, preferred_element_type=F32))
    so = []
    for p in pairs:
        mbd = jnp.where(same_head, o[p][:P, :P], 0.0) + jnp.where(eye2, p_end[:, sls[p]], 0.0)
        qn = r_t[:, sls[p]] + jnp.where(first, o[p][P:P + L, :P], o[p][P + L:, :P])
        so.append(jnp.dot(jnp.concatenate([qn, mbd], axis=0).astype(BF16), state_ref[p].astype(BF16),
                          preferred_element_type=F32))
    y = []
    for p in pairs:
        state_ref[p] = so[p][L:] + jnp.where(same_head, o[p][:P, P:], 0.0)
        y.append(so[p][:L] + jnp.where(first, o[p][P:P + L, P:], o[p][P + L:, P:]))
    mu = [_pair_sum(y[p], ones2) * (1.0 / HEAD) for p in pairs]
    yc = [y[p] - mu[p] for p in pairs]
    var = [_pair_sum(yc[p] * yc[p], ones2) * (1.0 / HEAD) for p in pairs]
    bonus = [_pair_sum(bonus_w[:, sls[p]], ones2) * v[:, sls[p]] for p in pairs]
    for p in pairs:
        s = sls[p]
        y_out[:, s] = yc[p] * lax.rsqrt(var[p] + GN_EPS) * gng_ref[:, s] + gnb_ref[:, s] + bonus[p]


def _post_kernel(tiles_per_seq,
                 x_ref, lng_ref, lnb_ref, ca_ref, c1_ref, yb_ref, wo_ref, l1g_ref, l1b_ref,
                 wup_ref, cw_ref, cb_ref, wdn_ref, l2g_ref, l2b_ref,
                 out_ref, carry_ref):
    i = pl.program_id(0)

    @pl.when(i % tiles_per_seq == 0)
    def _():
        carry_ref[...] = jnp.zeros_like(carry_ref)

    h = _ln(x_ref[...], lng_ref[...], lnb_ref[...], LN_EPS)
    mix = ca_ref[...] + c1_ref[...] * yb_ref[...]
    x1 = _ln(ALPHA * h + _bdot(mix, wo_ref[...]), l1g_ref[...], l1b_ref[...], LN_EPS)
    up = _bdot(x1, wup_ref[...])
    gate = up[:, :D_FF]
    val = up[:, D_FF:]
    carry = carry_ref[...]
    g1 = _shift_rows(gate, carry, 1)
    g2 = _shift_rows(gate, carry, 2)
    carry_ref[...] = gate[POST_ROWS - 8:, :]
    cw = cw_ref[...]
    conv = cb_ref[...] + g2 * cw[0:1, :] + g1 * cw[1:2, :] + gate * cw[2:3, :]
    ffn = _bdot(_gelu(conv) * val, wdn_ref[...])
    out_ref[...] = _ln(ALPHA * x1 + ffn, l2g_ref[...], l2b_ref[...], LN_EPS)


def _const_spec(shape):
    nd = len(shape)
    return pl.BlockSpec(shape, lambda *_: (0,) * nd, pipeline_mode=pl.Buffered(1))


def kernel(x, ln_in_g, ln_in_b, w_in, b_gate, mu_shift, sgu_ln_g, sgu_ln_b, w_s, b_s, w0, w2, a0, a2, g2, k_k, k_a, r_k, lnx_g, lnx_b, w_o, ln1_g, ln1_b, w_up, conv_w, conv_b, w_down, ln2_g, ln2_b):
    B, S, D = x.shape
    assert D == D_MODEL and w_in.shape[0] == 1
    T = B * S
    row = lambda p: p.reshape(1, -1).astype(F32)
    xf = x.reshape(T, D)

    wi = w_in[0]
    c_rk = 2 * D
    c_lora = c_rk + 3 * D
    c_gd = c_lora + DECAY_LORA + ICLR_LORA
    c_gates = c_gd + GATE_LORA
    w_uv = wi[:, :c_rk].astype(BF16)
    w_rk = jnp.concatenate(
        [wi[:, c_rk:c_gates], jnp.zeros((D, GATE_LORA_PAD - GATE_LORA), F32)], axis=1).astype(BF16)
    w_g = wi[:, c_gates:].astype(BF16)
    mu = jnp.concatenate([mu_shift[0], jnp.zeros((GATE_LORA_PAD - GATE_LORA,), F32)]).reshape(1, -1)
    w2p = jnp.concatenate([w2[0], jnp.zeros((ICLR_LORA, D), F32)], axis=0).astype(BF16)
    a2p = jnp.concatenate([jnp.zeros((DECAY_LORA, D), F32), a2[0]], axis=0).astype(BF16)
    g2p = jnp.concatenate([g2[0], jnp.zeros((GATE_LORA_PAD - GATE_LORA, D), F32)], axis=0).astype(BF16)
    lane = jnp.arange(128) // HEAD
    ones_blk = (lane[:, None] == lane[None, :]).astype(BF16)

    tiles_per_seq = S // PRE_ROWS
    tok = pl.BlockSpec((PRE_ROWS, D), lambda i: (i, 0))
    pre_in = [xf, row(ln_in_g), row(ln_in_b), w_uv, w_rk, w_g, row(b_gate), mu,
              row(sgu_ln_g), row(sgu_ln_b), w_s[0], b_s[0].T, row(w0), w2p, row(a0), a2p, g2p,
              row(k_k), row(k_a), ones_blk]
    pre_specs = [tok] + [_const_spec(a.shape) for a in pre_in[1:]]
    tok_shape = jax.ShapeDtypeStruct((T, D), F32)
    r_t, lw_t, k_t, v_t, kk_t, a_t, ca_t, c1_t = pl.pallas_call(
        functools.partial(_pre_kernel, tiles_per_seq),
        grid=(T // PRE_ROWS,),
        in_specs=pre_specs,
        out_specs=[tok] * 8,
        out_shape=[tok_shape] * 8,
        scratch_shapes=[pltpu.VMEM((8, RK_COLS), F32)],
        compiler_params=pltpu.CompilerParams(dimension_semantics=("arbitrary",),
                                             vmem_limit_bytes=VMEM_LIMIT),
        name="pre",
    )(*pre_in)

    n_chunks = S // CHUNK
    ctok = pl.BlockSpec((CHUNK, D), lambda b, c: (b * n_chunks + c, 0))
    frame = jnp.arange(CHUNK)
    tri = (frame[:, None] >= frame[None, :]).astype(BF16)
    tri2 = jnp.concatenate([tri, tri], axis=1)
    ones2 = jnp.concatenate([ones_blk, ones_blk], axis=0)
    scan_in = [r_t, lw_t, k_t, v_t, kk_t, a_t, row(r_k), row(lnx_g), row(lnx_b), tri2, ones2]
    yb_t = pl.pallas_call(
        _scan_kernel,
        grid=(B, n_chunks),
        in_specs=[ctok] * 6 + [_const_spec(a.shape) for a in scan_in[6:]],
        out_specs=ctok,
        out_shape=tok_shape,
        scratch_shapes=[pltpu.VMEM((HEADS // 2, 2 * HEAD, 2 * HEAD), F32)],
        compiler_params=pltpu.CompilerParams(dimension_semantics=("arbitrary", "arbitrary"),
                                             vmem_limit_bytes=VMEM_LIMIT),
        name="scan",
    )(*scan_in)

    ptok = pl.BlockSpec((POST_ROWS, D), lambda i: (i, 0))
    post_in = [xf, row(ln_in_g), row(ln_in_b), ca_t, c1_t, yb_t, w_o[0].astype(BF16),
               row(ln1_g), row(ln1_b), w_up[0].astype(BF16), conv_w[0], row(conv_b),
               w_down[0].astype(BF16), row(ln2_g), row(ln2_b)]
    post_specs = [ptok, _const_spec((1, D)), _const_spec((1, D)), ptok, ptok, ptok] + \
                 [_const_spec(a.shape) for a in post_in[6:]]
    out = pl.pallas_call(
        functools.partial(_post_kernel, S // POST_ROWS),
        grid=(T // POST_ROWS,),
        in_specs=post_specs,
        out_specs=ptok,
        out_shape=tok_shape,
        scratch_shapes=[pltpu.VMEM((8, D_FF), F32)],
        compiler_params=pltpu.CompilerParams(dimension_semantics=("arbitrary",),
                                             vmem_limit_bytes=VMEM_LIMIT),
        name="post",
    )(*post_in)
    return out.reshape(B, S, D)
```

```python
import functools
import math

import jax
import jax.numpy as jnp
from jax import lax
from jax.experimental import pallas as pl
from jax.experimental.pallas import tpu as pltpu

D_MODEL = 1024
SGU_BLOCK = 128
SGU_GROUPS = 8
HEAD = 64
HEADS = D_MODEL // HEAD
DECAY_LORA = 64
ICLR_LORA = 64
GATE_LORA = 160
GATE_LORA_PAD = 256
D_FF = 2688
CHUNK = 64
LN_EPS = 1e-5
GN_EPS = 64e-5
ALPHA = 2.0 ** 0.25
RK_COLS = 3 * D_MODEL + 128 + GATE_LORA_PAD

PRE_ROWS = 256
POST_ROWS = 256
VMEM_LIMIT = 56 * 1024 * 1024

F32 = jnp.float32
BF16 = jnp.bfloat16


def _ln(x, g, b, eps):
    mu = jnp.mean(x, -1, keepdims=True)
    xc = x - mu
    var = jnp.mean(xc * xc, -1, keepdims=True)
    return xc * lax.rsqrt(var + eps) * g + b


def _gelu(x):
    return 0.5 * x * (1.0 + lax.erf(x * (1.0 / math.sqrt(2.0))))


def _sigmoid(x):
    return 1.0 / (1.0 + jnp.exp(-x))


def _bdot(a, b):
    return jnp.dot(a.astype(BF16), b.astype(BF16), preferred_element_type=F32)


def _head_sum(q, ones_blk):
    q_hi = q.astype(BF16)
    q_lo = (q - q_hi.astype(F32)).astype(BF16)
    cols = []
    for g in range(q.shape[1] // 128):
        sl = slice(g * 128, (g + 1) * 128)
        cols.append(jnp.dot(q_hi[:, sl], ones_blk, preferred_element_type=F32)
                    + jnp.dot(q_lo[:, sl], ones_blk, preferred_element_type=F32))
    return jnp.concatenate(cols, axis=1)


def _shift_rows(x, carry, k):
    rolled = pltpu.roll(x, k, 0)
    prev = pltpu.roll(carry, k, 0)
    row = lax.broadcasted_iota(jnp.int32, x.shape, 0)
    head = jnp.concatenate([prev, rolled[8:]], axis=0)
    return jnp.where(row < k, head, rolled)


def _pre_kernel(tiles_per_seq,
                x_ref, lng_ref, lnb_ref, wuv_ref, wrk_ref, wg_ref, bg_ref, mu_ref,
                sg_ref, sb_ref, ws_ref, bst_ref, w0_ref, w2_ref, a0_ref, a2_ref, g2_ref,
                kk_ref, ka_ref, ones_ref,
                r_out, lw_out, k_out, v_out, kk_out, a_out, ca_out, c1_out,
                carry_ref):
    i = pl.program_id(0)

    @pl.when(i % tiles_per_seq == 0)
    def _():
        carry_ref[...] = jnp.zeros_like(carry_ref)

    h = _ln(x_ref[...], lng_ref[...], lnb_ref[...], LN_EPS)
    hb = h.astype(BF16)

    uv = jnp.dot(hb, wuv_ref[...], preferred_element_type=F32)
    u = _gelu(uv[:, :D_MODEL])
    vn = _ln(_gelu(uv[:, D_MODEL:]), sg_ref[...], sb_ref[...], LN_EPS).astype(BF16)
    rr = lax.broadcasted_iota(jnp.int32, (SGU_BLOCK, SGU_BLOCK), 0) // CHUNK
    cc = lax.broadcasted_iota(jnp.int32, (SGU_BLOCK, SGU_BLOCK), 1) // CHUNK
    blk_mask = rr >= cc
    bst = bst_ref[...]
    z_rows = []
    for nb in range(PRE_ROWS // SGU_BLOCK):
        z_cols = []
        for g in range(SGU_GROUPS):
            wm = jnp.where(blk_mask, ws_ref[g], 0.0).astype(BF16)
            vb = vn[nb * SGU_BLOCK:(nb + 1) * SGU_BLOCK, g * 128:(g + 1) * 128]
            z_cols.append(jnp.dot(wm, vb, preferred_element_type=F32) + bst[:, g:g + 1])
        z_rows.append(jnp.concatenate(z_cols, axis=1))
    ya = u * jnp.concatenate(z_rows, axis=0)

    gates = _sigmoid(jnp.dot(hb, wg_ref[...], preferred_element_type=F32) + bg_ref[...])
    ca_out[...] = gates[:, :D_MODEL] * ya
    gb = gates[:, D_MODEL:]

    rk = jnp.dot(hb, wrk_ref[...], preferred_element_type=F32)
    rk_prev = _shift_rows(rk, carry_ref[...], 1)
    carry_ref[...] = rk[PRE_ROWS - 8:, :]
    rk = rk + (rk_prev - rk) * mu_ref[...]
    r = rk[:, :D_MODEL]
    k = rk[:, D_MODEL:2 * D_MODEL]
    val = rk[:, 2 * D_MODEL:3 * D_MODEL]
    wdad = rk[:, 3 * D_MODEL:3 * D_MODEL + 128]
    gd = rk[:, 3 * D_MODEL + 128:]

    zw = w0_ref[...] + _bdot(jnp.tanh(wdad), w2_ref[...])
    lw_out[...] = (-math.exp(-0.5)) * _sigmoid(zw)
    a_lr = _sigmoid(a0_ref[...] + _bdot(wdad, a2_ref[...]))
    g = _bdot(_sigmoid(gd), g2_ref[...])
    c1_out[...] = gb * g

    kk = k * kk_ref[...]
    nrm = jnp.sqrt(_head_sum(kk * kk, ones_ref[...]))
    kk_out[...] = kk / jnp.maximum(nrm, 1e-12)
    k_out[...] = k * (1.0 + (a_lr - 1.0) * ka_ref[...])
    r_out[...] = r
    v_out[...] = val
    a_out[...] = a_lr


def _dot_nt(a, b):
    return lax.dot_general(a, b, (((1,), (1,)), ((), ())), preferred_element_type=F32)


def _split(x):
    hi = x.astype(BF16)
    return hi, (x - hi.astype(F32)).astype(BF16)


def _pair_sum(q, ones2):
    hi, lo = _split(q)
    return jnp.dot(jnp.concatenate([hi, lo], axis=1), ones2, preferred_element_type=F32)


def _scan_kernel(r_ref, lw_ref, k_ref, v_ref, kk_ref, a_ref, rk_ref, gng_ref, gnb_ref,
                 tri2_ref, ones2_ref, y_out, state_ref):
    c = pl.program_id(1)

    @pl.when(c == 0)
    def _():
        state_ref[...] = jnp.zeros_like(state_ref)

    L = CHUNK
    P = 2 * HEAD
    lw = lw_ref[...]
    lw_hi, lw_lo = _split(lw)
    cs = jnp.dot(tri2_ref[...], jnp.concatenate([lw_hi, lw_lo], axis=0), preferred_element_type=F32)
    e_pos = jnp.exp(cs)
    e_neg = jnp.exp(-cs)
    p_end = e_pos[L - 1:L, :]
    r = r_ref[...]
    kh = k_ref[...]
    v = v_ref[...]
    kk = kk_ref[...]
    a_t = -kk * jnp.exp(cs - lw)
    r_t = r * e_pos
    b_t = kk * a_ref[...] * e_neg
    k_t = kh * e_neg
    b_h = b_t * p_end
    k_h = k_t * p_end
    bonus_w = r * kh * rk_ref[...]

    t_row = lax.broadcasted_iota(jnp.int32, (L, P), 0)
    lane = lax.broadcasted_iota(jnp.int32, (L, P), 1)
    first = lane < HEAD
    s_col = jnp.where(first, lane, lane - HEAD)
    strict = s_col < t_row
    incl = s_col <= t_row
    diag = s_col == t_row
    first2 = jnp.concatenate([first, first], axis=1)
    first3 = jnp.concatenate([first, first, first], axis=1)
    zero_b = jnp.zeros((L, P), BF16)
    ones2 = ones2_ref[...]

    def blockdiag(x, mask):
        return jnp.concatenate([jnp.where(mask, x, 0), jnp.where(mask, 0, x)], axis=0)

    pairs = range(HEADS // 2)
    sls = [slice(p * P, (p + 1) * P) for p in pairs]
    at1 = [jnp.where(first, a_t[:, s], 0.0) for s in sls]
    at2 = [jnp.where(first, 0.0, a_t[:, s]) for s in sls]
    rt1 = [jnp.where(first, r_t[:, s], 0.0) for s in sls]
    rt2 = [jnp.where(first, 0.0, r_t[:, s]) for s in sls]
    bt_b = [b_t[:, s].astype(BF16) for s in sls]
    kt_b = [k_t[:, s].astype(BF16) for s in sls]
    vbd = [blockdiag(v[:, s], first).astype(BF16) for s in sls]
    ga = [_dot_nt(jnp.concatenate([rt1[p], rt2[p], at1[p]], axis=0).astype(BF16),
                  jnp.concatenate([bt_b[p], kt_b[p]], axis=0)) for p in pairs]
    gb = [_dot_nt(at2[p].astype(BF16), jnp.concatenate([kt_b[p], bt_b[p]], axis=0)) for p in pairs]
    g1 = [jnp.where(strict, ga[p][2 * L:], 0.0) for p in pairs]
    g2 = [jnp.where(strict, gb[p], 0.0) for p in pairs]
    npow = [jnp.where(first, g1[p], g2[p]) for p in pairs]
    ak_sw = [jnp.where(first, g2[p], g1[p]) for p in pairs]
    zu = [jnp.dot(ak_sw[p].astype(BF16), jnp.concatenate([vbd[p][L:], vbd[p][:L]], axis=0),
                  preferred_element_type=F32) for p in pairs]
    z = [jnp.concatenate([a_t[:, sls[p]], zu[p]], axis=1) for p in pairs]
    for lev in range(6):
        if lev < 5:
            o = [jnp.dot(npow[p].astype(BF16),
                         blockdiag(jnp.concatenate([npow[p], z[p]], axis=1).astype(BF16), first3),
                         preferred_element_type=F32) for p in pairs]
            npow = [o[p][:, :P] for p in pairs]
            z = [z[p] + o[p][:, P:] for p in pairs]
        else:
            z = [z[p] + jnp.dot(npow[p].astype(BF16), blockdiag(z[p].astype(BF16), first2),
                                preferred_element_type=F32) for p in pairs]
    o = []
    for p in pairs:
        zb = z[p].astype(BF16)
        rhs = jnp.concatenate([jnp.where(first2, zb, 0),
                               jnp.concatenate([zero_b, vbd[p][:L]], axis=1),
                               jnp.where(first2, 0, zb),
                               jnp.concatenate([zero_b, vbd[p][L:]], axis=1)], axis=0)
        t_all = jnp.concatenate([b_h[:, sls[p]], k_h[:, sls[p]]], axis=0).T
        lhs = jnp.concatenate(
            [jnp.concatenate([t_all[:L], t_all[L:]], axis=1),
             jnp.concatenate([jnp.where(incl, ga[p][:L], 0.0), jnp.where(incl, ga[p][L:2 * L], 0.0)], axis=1)],
            axis=0).astype(BF16)
        o.append(jnp.dot(lhs, rhs, preferred_element_type=F32))
    so = []
    for p in pairs:
        m_nat = o[p][:L, :P] + jnp.where(diag, p_end[:, sls[p]], 0.0)
        q_nat = r_t[:, sls[p]] + o[p][L:, :P]
        so.append(jnp.dot(jnp.concatenate([q_nat, m_nat], axis=0).astype(BF16),
                          blockdiag(state_ref[p].astype(BF16), first), preferred_element_type=F32))
    y = []
    for p in pairs:
        state_ref[p] = so[p][L:] + o[p][:L, P:]
        y.append(so[p][:L] + o[p][L:, P:])
    mu = [_pair_sum(y[p], ones2) * (1.0 / HEAD) for p in pairs]
    yc = [y[p] - mu[p] for p in pairs]
    var = [_pair_sum(yc[p] * yc[p], ones2) * (1.0 / HEAD) for p in pairs]
    bonus = [_pair_sum(bonus_w[:, sls[p]], ones2) * v[:, sls[p]] for p in pairs]
    for p in pairs:
        s = sls[p]
        y_out[:, s] = yc[p] * lax.rsqrt(var[p] + GN_EPS) * gng_ref[:, s] + gnb_ref[:, s] + bonus[p]


def _post_kernel(tiles_per_seq,
                 x_ref, lng_ref, lnb_ref, ca_ref, c1_ref, yb_ref, wo_ref, l1g_ref, l1b_ref,
                 wup_ref, cw_ref, cb_ref, wdn_ref, l2g_ref, l2b_ref,
                 out_ref, carry_ref):
    i = pl.program_id(0)

    @pl.when(i % tiles_per_seq == 0)
    def _():
        carry_ref[...] = jnp.zeros_like(carry_ref)

    h = _ln(x_ref[...], lng_ref[...], lnb_ref[...], LN_EPS)
    mix = ca_ref[...] + c1_ref[...] * yb_ref[...]
    x1 = _ln(ALPHA * h + _bdot(mix, wo_ref[...]), l1g_ref[...], l1b_ref[...], LN_EPS)
    up = _bdot(x1, wup_ref[...])
    gate = up[:, :D_FF]
    val = up[:, D_FF:]
    carry = carry_ref[...]
    g1 = _shift_rows(gate, carry, 1)
    g2 = _shift_rows(gate, carry, 2)
    carry_ref[...] = gate[POST_ROWS - 8:, :]
    cw = cw_ref[...]
    conv = cb_ref[...] + g2 * cw[0:1, :] + g1 * cw[1:2, :] + gate * cw[2:3, :]
    ffn = _bdot(_gelu(conv) * val, wdn_ref[...])
    out_ref[...] = _ln(ALPHA * x1 + ffn, l2g_ref[...], l2b_ref[...], LN_EPS)


def _const_spec(shape):
    nd = len(shape)
    return pl.BlockSpec(shape, lambda *_: (0,) * nd, pipeline_mode=pl.Buffered(1))


def kernel(x, ln_in_g, ln_in_b, w_in, b_gate, mu_shift, sgu_ln_g, sgu_ln_b, w_s, b_s, w0, w2, a0, a2, g2, k_k, k_a, r_k, lnx_g, lnx_b, w_o, ln1_g, ln1_b, w_up, conv_w, conv_b, w_down, ln2_g, ln2_b):
    B, S, D = x.shape
    assert D == D_MODEL and w_in.shape[0] == 1
    T = B * S
    row = lambda p: p.reshape(1, -1).astype(F32)
    xf = x.reshape(T, D)

    wi = w_in[0]
    c_rk = 2 * D
    c_lora = c_rk + 3 * D
    c_gd = c_lora + DECAY_LORA + ICLR_LORA
    c_gates = c_gd + GATE_LORA
    w_uv = wi[:, :c_rk].astype(BF16)
    w_rk = jnp.concatenate(
        [wi[:, c_rk:c_gates], jnp.zeros((D, GATE_LORA_PAD - GATE_LORA), F32)], axis=1).astype(BF16)
    w_g = wi[:, c_gates:].astype(BF16)
    mu = jnp.concatenate([mu_shift[0], jnp.zeros((GATE_LORA_PAD - GATE_LORA,), F32)]).reshape(1, -1)
    w2p = jnp.concatenate([w2[0], jnp.zeros((ICLR_LORA, D), F32)], axis=0).astype(BF16)
    a2p = jnp.concatenate([jnp.zeros((DECAY_LORA, D), F32), a2[0]], axis=0).astype(BF16)
    g2p = jnp.concatenate([g2[0], jnp.zeros((GATE_LORA_PAD - GATE_LORA, D), F32)], axis=0).astype(BF16)
    lane = jnp.arange(128) // HEAD
    ones_blk = (lane[:, None] == lane[None, :]).astype(BF16)

    tiles_per_seq = S // PRE_ROWS
    tok = pl.BlockSpec((PRE_ROWS, D), lambda i: (i, 0))
    pre_in = [xf, row(ln_in_g), row(ln_in_b), w_uv, w_rk, w_g, row(b_gate), mu,
              row(sgu_ln_g), row(sgu_ln_b), w_s[0], b_s[0].T, row(w0), w2p, row(a0), a2p, g2p,
              row(k_k), row(k_a), ones_blk]
    pre_specs = [tok] + [_const_spec(a.shape) for a in pre_in[1:]]
    tok_shape = jax.ShapeDtypeStruct((T, D), F32)
    r_t, lw_t, k_t, v_t, kk_t, a_t, ca_t, c1_t = pl.pallas_call(
        functools.partial(_pre_kernel, tiles_per_seq),
        grid=(T // PRE_ROWS,),
        in_specs=pre_specs,
        out_specs=[tok] * 8,
        out_shape=[tok_shape] * 8,
        scratch_shapes=[pltpu.VMEM((8, RK_COLS), F32)],
        compiler_params=pltpu.CompilerParams(dimension_semantics=("arbitrary",),
                                             vmem_limit_bytes=VMEM_LIMIT),
        name="pre",
    )(*pre_in)

    n_chunks = S // CHUNK
    ctok = pl.BlockSpec((CHUNK, D), lambda b, c: (b * n_chunks + c, 0))
    frame = jnp.arange(CHUNK)
    tri = (frame[:, None] >= frame[None, :]).astype(BF16)
    tri2 = jnp.concatenate([tri, tri], axis=1)
    ones2 = jnp.concatenate([ones_blk, ones_blk], axis=0)
    scan_in = [r_t, lw_t, k_t, v_t, kk_t, a_t, row(r_k), row(lnx_g), row(lnx_b), tri2, ones2]
    yb_t = pl.pallas_call(
        _scan_kernel,
        grid=(B, n_chunks),
        in_specs=[ctok] * 6 + [_const_spec(a.shape) for a in scan_in[6:]],
        out_specs=ctok,
        out_shape=tok_shape,
        scratch_shapes=[pltpu.VMEM((HEADS // 2, HEAD, 2 * HEAD), F32)],
        compiler_params=pltpu.CompilerParams(dimension_semantics=("arbitrary", "arbitrary"),
                                             vmem_limit_bytes=VMEM_LIMIT),
        name="scan",
    )(*scan_in)

    ptok = pl.BlockSpec((POST_ROWS, D), lambda i: (i, 0))
    post_in = [xf, row(ln_in_g), row(ln_in_b), ca_t, c1_t, yb_t, w_o[0].astype(BF16),
               row(ln1_g), row(ln1_b), w_up[0].astype(BF16), conv_w[0], row(conv_b),
               w_down[0].astype(BF16), row(ln2_g), row(ln2_b)]
    post_specs = [ptok, _const_spec((1, D)), _const_spec((1, D)), ptok, ptok, ptok] + \
                 [_const_spec(a.shape) for a in post_in[6:]]
    out = pl.pallas_call(
        functools.partial(_post_kernel, S // POST_ROWS),
        grid=(T // POST_ROWS,),
        in_specs=post_specs,
        out_specs=ptok,
        out_shape=tok_shape,
        scratch_shapes=[pltpu.VMEM((8, D_FF), F32)],
        compiler_params=pltpu.CompilerParams(dimension_semantics=("arbitrary",),
                                             vmem_limit_bytes=VMEM_LIMIT),
        name="post",
    )(*post_in)
    return out.reshape(B, S, D)
```

```python
import functools
import math

import jax
import jax.numpy as jnp
from jax import lax
from jax.experimental import pallas as pl
from jax.experimental.pallas import tpu as pltpu

D_MODEL = 1024
SGU_BLOCK = 128
SGU_GROUPS = 8
HEAD = 64
HEADS = D_MODEL // HEAD
DECAY_LORA = 64
ICLR_LORA = 64
GATE_LORA = 160
GATE_LORA_PAD = 256
D_FF = 2688
CHUNK = 64
LN_EPS = 1e-5
GN_EPS = 64e-5
ALPHA = 2.0 ** 0.25
RK_COLS = 3 * D_MODEL + 128 + GATE_LORA_PAD

PRE_ROWS = 256
POST_ROWS = 256
VMEM_LIMIT = 56 * 1024 * 1024

F32 = jnp.float32
BF16 = jnp.bfloat16


def _ln(x, g, b, eps):
    mu = jnp.mean(x, -1, keepdims=True)
    xc = x - mu
    var = jnp.mean(xc * xc, -1, keepdims=True)
    return xc * lax.rsqrt(var + eps) * g + b


def _gelu(x):
    return 0.5 * x * (1.0 + lax.erf(x * (1.0 / math.sqrt(2.0))))


def _sigmoid(x):
    return 1.0 / (1.0 + jnp.exp(-x))


def _bdot(a, b):
    return jnp.dot(a.astype(BF16), b.astype(BF16), preferred_element_type=F32)


def _head_sum(q, ones_blk):
    q_hi = q.astype(BF16)
    q_lo = (q - q_hi.astype(F32)).astype(BF16)
    cols = []
    for g in range(q.shape[1] // 128):
        sl = slice(g * 128, (g + 1) * 128)
        cols.append(jnp.dot(q_hi[:, sl], ones_blk, preferred_element_type=F32)
                    + jnp.dot(q_lo[:, sl], ones_blk, preferred_element_type=F32))
    return jnp.concatenate(cols, axis=1)


def _shift_rows(x, carry, k):
    rolled = pltpu.roll(x, k, 0)
    prev = pltpu.roll(carry, k, 0)
    row = lax.broadcasted_iota(jnp.int32, x.shape, 0)
    head = jnp.concatenate([prev, rolled[8:]], axis=0)
    return jnp.where(row < k, head, rolled)


def _pre_kernel(tiles_per_seq,
                x_ref, lng_ref, lnb_ref, wuv_ref, wrk_ref, wg_ref, bg_ref, mu_ref,
                sg_ref, sb_ref, ws_ref, bst_ref, w0_ref, w2_ref, a0_ref, a2_ref, g2_ref,
                kk_ref, ka_ref, ones_ref,
                r_out, lw_out, k_out, v_out, kk_out, a_out, ca_out, c1_out,
                carry_ref):
    i = pl.program_id(0)

    @pl.when(i % tiles_per_seq == 0)
    def _():
        carry_ref[...] = jnp.zeros_like(carry_ref)

    h = _ln(x_ref[...], lng_ref[...], lnb_ref[...], LN_EPS)
    hb = h.astype(BF16)

    uv = jnp.dot(hb, wuv_ref[...], preferred_element_type=F32)
    u = _gelu(uv[:, :D_MODEL])
    vn = _ln(_gelu(uv[:, D_MODEL:]), sg_ref[...], sb_ref[...], LN_EPS).astype(BF16)
    rr = lax.broadcasted_iota(jnp.int32, (SGU_BLOCK, SGU_BLOCK), 0) // CHUNK
    cc = lax.broadcasted_iota(jnp.int32, (SGU_BLOCK, SGU_BLOCK), 1) // CHUNK
    blk_mask = rr >= cc
    bst = bst_ref[...]
    z_rows = []
    for nb in range(PRE_ROWS // SGU_BLOCK):
        z_cols = []
        for g in range(SGU_GROUPS):
            wm = jnp.where(blk_mask, ws_ref[g], 0.0).astype(BF16)
            vb = vn[nb * SGU_BLOCK:(nb + 1) * SGU_BLOCK, g * 128:(g + 1) * 128]
            z_cols.append(jnp.dot(wm, vb, preferred_element_type=F32) + bst[:, g:g + 1])
        z_rows.append(jnp.concatenate(z_cols, axis=1))
    ya = u * jnp.concatenate(z_rows, axis=0)

    gates = _sigmoid(jnp.dot(hb, wg_ref[...], preferred_element_type=F32) + bg_ref[...])
    ca_out[...] = gates[:, :D_MODEL] * ya
    gb = gates[:, D_MODEL:]

    rk = jnp.dot(hb, wrk_ref[...], preferred_element_type=F32)
    rk_prev = _shift_rows(rk, carry_ref[...], 1)
    carry_ref[...] = rk[PRE_ROWS - 8:, :]
    rk = rk + (rk_prev - rk) * mu_ref[...]
    r = rk[:, :D_MODEL]
    k = rk[:, D_MODEL:2 * D_MODEL]
    val = rk[:, 2 * D_MODEL:3 * D_MODEL]
    wdad = rk[:, 3 * D_MODEL:3 * D_MODEL + 128]
    gd = rk[:, 3 * D_MODEL + 128:]

    zw = w0_ref[...] + _bdot(jnp.tanh(wdad), w2_ref[...])
    lw_out[...] = (-math.exp(-0.5)) * _sigmoid(zw)
    a_lr = _sigmoid(a0_ref[...] + _bdot(wdad, a2_ref[...]))
    g = _bdot(_sigmoid(gd), g2_ref[...])
    c1_out[...] = gb * g

    kk = k * kk_ref[...]
    nrm = jnp.sqrt(_head_sum(kk * kk, ones_ref[...]))
    kk_out[...] = kk / jnp.maximum(nrm, 1e-12)
    k_out[...] = k * (1.0 + (a_lr - 1.0) * ka_ref[...])
    r_out[...] = r
    v_out[...] = val
    a_out[...] = a_lr


def _dot_nt(a, b):
    return lax.dot_general(a, b, (((1,), (1,)), ((), ())), preferred_element_type=F32)


def _split(x):
    hi = x.astype(BF16)
    return hi, (x - hi.astype(F32)).astype(BF16)


def _pair_sum(q, ones2):
    hi, lo = _split(q)
    return jnp.dot(jnp.concatenate([hi, lo], axis=1), ones2, preferred_element_type=F32)


def _scan_kernel(r_ref, lw_ref, k_ref, v_ref, kk_ref, a_ref, rk_ref, gng_ref, gnb_ref,
                 tri2_ref, ones2_ref, y_out, state_ref):
    c = pl.program_id(0)

    @pl.when(c == 0)
    def _():
        state_ref[...] = jnp.zeros_like(state_ref)

    L = CHUNK
    P = 2 * HEAD
    NB = r_ref.shape[0]
    wide = lambda ref: jnp.concatenate([ref[b] for b in range(NB)], axis=1)
    tiled = lambda ref: jnp.concatenate([ref[...]] * NB, axis=1)
    lw = wide(lw_ref)
    lw_hi, lw_lo = _split(lw)
    cs = jnp.dot(tri2_ref[...], jnp.concatenate([lw_hi, lw_lo], axis=0), preferred_element_type=F32)
    e_pos = jnp.exp(cs)
    e_neg = jnp.exp(-cs)
    p_end = e_pos[L - 1:L, :]
    r = wide(r_ref)
    kh = wide(k_ref)
    v = wide(v_ref)
    kk = wide(kk_ref)
    a_t = -kk * jnp.exp(cs - lw)
    r_t = r * e_pos
    b_t = kk * wide(a_ref) * e_neg
    k_t = kh * e_neg
    b_h = b_t * p_end
    k_h = k_t * p_end
    bonus_w = r * kh * tiled(rk_ref)
    gn_g = tiled(gng_ref)
    gn_b = tiled(gnb_ref)

    t_row = lax.broadcasted_iota(jnp.int32, (L, P), 0)
    lane = lax.broadcasted_iota(jnp.int32, (L, P), 1)
    first = lane < HEAD
    s_col = jnp.where(first, lane, lane - HEAD)
    strict = s_col < t_row
    incl = s_col <= t_row
    diag = s_col == t_row
    ones2 = ones2_ref[...]

    def bd(x):
        return jnp.concatenate([jnp.where(first, x, 0), jnp.where(first, 0, x)], axis=0)

    def nat_t(x):
        t = bd(x).T
        return t[:L] + t[L:]

    pairs = range(NB * HEADS // 2)
    sls = [slice(p * P, (p + 1) * P) for p in pairs]
    at_b = [a_t[:, s].astype(BF16) for s in sls]
    rt_b = [r_t[:, s].astype(BF16) for s in sls]
    vbd = [bd(v[:, s].astype(BF16)) for s in sls]
    ar = [jnp.concatenate([at_b[p], rt_b[p]], axis=0) for p in pairs]
    gb = [_dot_nt(ar[p], bd(b_t[:, sls[p]].astype(BF16))) for p in pairs]
    gk = [_dot_nt(ar[p], bd(k_t[:, sls[p]].astype(BF16))) for p in pairs]
    a_ak = [jnp.where(strict, gk[p][:L], 0.0).astype(BF16) for p in pairs]
    a_rb = [jnp.where(incl, gb[p][L:], 0.0).astype(BF16) for p in pairs]
    a_rk = [jnp.where(incl, gk[p][L:], 0.0).astype(BF16) for p in pairs]
    n = [jnp.where(strict, gb[p][:L], 0.0) for p in pairs]
    x = [jnp.where(diag, 1.0, 0.0) + n[p] for p in pairs]
    n = [jnp.dot(n[p].astype(BF16), bd(n[p].astype(BF16)), preferred_element_type=F32) for p in pairs]
    for _ in range(4):
        o = [jnp.dot(jnp.concatenate([n[p], x[p]], axis=0).astype(BF16), bd(n[p].astype(BF16)),
                     preferred_element_type=F32) for p in pairs]
        n = [o[p][:L] for p in pairs]
        x = [x[p] + o[p][L:] for p in pairs]
    x = [(x[p] + jnp.dot(x[p].astype(BF16), bd(n[p].astype(BF16)), preferred_element_type=F32)).astype(BF16)
         for p in pairs]
    bk_t = [jnp.concatenate([nat_t(b_h[:, s]), nat_t(k_h[:, s])], axis=1).astype(BF16) for s in sls]
    p_t = [_pair_sum(jnp.where(diag, p_end[:, s], 0.0), ones2) for s in sls]

    hbd = [bd(state_ref[p].astype(BF16)) for p in pairs]
    u = [jnp.dot(jnp.concatenate([at_b[p], a_ak[p]], axis=1), jnp.concatenate([hbd[p], vbd[p]], axis=0),
                 preferred_element_type=F32) for p in pairs]
    u = [jnp.dot(x[p], bd(u[p].astype(BF16)), preferred_element_type=F32) for p in pairs]
    ubd = [bd(u[p].astype(BF16)) for p in pairs]
    y = [jnp.dot(jnp.concatenate([rt_b[p], a_rb[p], a_rk[p]], axis=1),
                 jnp.concatenate([hbd[p], ubd[p], vbd[p]], axis=0), preferred_element_type=F32) for p in pairs]
    for p in pairs:
        state_ref[p] = state_ref[p] * p_t[p] + jnp.dot(
            bk_t[p], jnp.concatenate([ubd[p], vbd[p]], axis=0), preferred_element_type=F32)
    mu = [_pair_sum(y[p], ones2) * (1.0 / HEAD) for p in pairs]
    yc = [y[p] - mu[p] for p in pairs]
    var = [_pair_sum(yc[p] * yc[p], ones2) * (1.0 / HEAD) for p in pairs]
    bonus = [_pair_sum(bonus_w[:, sls[p]], ones2) * v[:, sls[p]] for p in pairs]
    for p in pairs:
        s = sls[p]
        b, hp = divmod(p, HEADS // 2)
        y_out[b, :, hp * P:(hp + 1) * P] = (
            yc[p] * lax.rsqrt(var[p] + GN_EPS) * gn_g[:, s] + gn_b[:, s] + bonus[p])


def _post_kernel(tiles_per_seq,
                 x_ref, lng_ref, lnb_ref, ca_ref, c1_ref, yb_ref, wo_ref, l1g_ref, l1b_ref,
                 wup_ref, cw_ref, cb_ref, wdn_ref, l2g_ref, l2b_ref,
                 out_ref, carry_ref):
    i = pl.program_id(0)

    @pl.when(i % tiles_per_seq == 0)
    def _():
        carry_ref[...] = jnp.zeros_like(carry_ref)

    h = _ln(x_ref[...], lng_ref[...], lnb_ref[...], LN_EPS)
    mix = ca_ref[...] + c1_ref[...] * yb_ref[...]
    x1 = _ln(ALPHA * h + _bdot(mix, wo_ref[...]), l1g_ref[...], l1b_ref[...], LN_EPS)
    up = _bdot(x1, wup_ref[...])
    gate = up[:, :D_FF]
    val = up[:, D_FF:]
    carry = carry_ref[...]
    g1 = _shift_rows(gate, carry, 1)
    g2 = _shift_rows(gate, carry, 2)
    carry_ref[...] = gate[POST_ROWS - 8:, :]
    cw = cw_ref[...]
    conv = cb_ref[...] + g2 * cw[0:1, :] + g1 * cw[1:2, :] + gate * cw[2:3, :]
    ffn = _bdot(_gelu(conv) * val, wdn_ref[...])
    out_ref[...] = _ln(ALPHA * x1 + ffn, l2g_ref[...], l2b_ref[...], LN_EPS)


def _const_spec(shape):
    nd = len(shape)
    return pl.BlockSpec(shape, lambda *_: (0,) * nd, pipeline_mode=pl.Buffered(1))


def kernel(x, ln_in_g, ln_in_b, w_in, b_gate, mu_shift, sgu_ln_g, sgu_ln_b, w_s, b_s, w0, w2, a0, a2, g2, k_k, k_a, r_k, lnx_g, lnx_b, w_o, ln1_g, ln1_b, w_up, conv_w, conv_b, w_down, ln2_g, ln2_b):
    B, S, D = x.shape
    assert D == D_MODEL and w_in.shape[0] == 1
    T = B * S
    row = lambda p: p.reshape(1, -1).astype(F32)
    xf = x.reshape(T, D)

    wi = w_in[0]
    c_rk = 2 * D
    c_lora = c_rk + 3 * D
    c_gd = c_lora + DECAY_LORA + ICLR_LORA
    c_gates = c_gd + GATE_LORA
    w_uv = wi[:, :c_rk].astype(BF16)
    w_rk = jnp.concatenate(
        [wi[:, c_rk:c_gates], jnp.zeros((D, GATE_LORA_PAD - GATE_LORA), F32)], axis=1).astype(BF16)
    w_g = wi[:, c_gates:].astype(BF16)
    mu = jnp.concatenate([mu_shift[0], jnp.zeros((GATE_LORA_PAD - GATE_LORA,), F32)]).reshape(1, -1)
    w2p = jnp.concatenate([w2[0], jnp.zeros((ICLR_LORA, D), F32)], axis=0).astype(BF16)
    a2p = jnp.concatenate([jnp.zeros((DECAY_LORA, D), F32), a2[0]], axis=0).astype(BF16)
    g2p = jnp.concatenate([g2[0], jnp.zeros((GATE_LORA_PAD - GATE_LORA, D), F32)], axis=0).astype(BF16)
    lane = jnp.arange(128) // HEAD
    ones_blk = (lane[:, None] == lane[None, :]).astype(BF16)

    tiles_per_seq = S // PRE_ROWS
    tok = pl.BlockSpec((PRE_ROWS, D), lambda i: (i, 0))
    pre_in = [xf, row(ln_in_g), row(ln_in_b), w_uv, w_rk, w_g, row(b_gate), mu,
              row(sgu_ln_g), row(sgu_ln_b), w_s[0], b_s[0].T, row(w0), w2p, row(a0), a2p, g2p,
              row(k_k), row(k_a), ones_blk]
    pre_specs = [tok] + [_const_spec(a.shape) for a in pre_in[1:]]
    tok_shape = jax.ShapeDtypeStruct((T, D), F32)
    r_t, lw_t, k_t, v_t, kk_t, a_t, ca_t, c1_t = pl.pallas_call(
        functools.partial(_pre_kernel, tiles_per_seq),
        grid=(T // PRE_ROWS,),
        in_specs=pre_specs,
        out_specs=[tok] * 8,
        out_shape=[tok_shape] * 8,
        scratch_shapes=[pltpu.VMEM((8, RK_COLS), F32)],
        compiler_params=pltpu.CompilerParams(dimension_semantics=("arbitrary",),
                                             vmem_limit_bytes=VMEM_LIMIT),
        name="pre",
    )(*pre_in)

    n_chunks = S // CHUNK
    ctok = pl.BlockSpec((B, CHUNK, D), lambda c: (0, c, 0))
    frame = jnp.arange(CHUNK)
    tri = (frame[:, None] >= frame[None, :]).astype(BF16)
    tri2 = jnp.concatenate([tri, tri], axis=1)
    ones2 = jnp.concatenate([ones_blk, ones_blk], axis=0)
    seq = lambda t: t.reshape(B, S, D)
    scan_in = [seq(r_t), seq(lw_t), seq(k_t), seq(v_t), seq(kk_t), seq(a_t),
               row(r_k), row(lnx_g), row(lnx_b), tri2, ones2]
    yb_t = pl.pallas_call(
        _scan_kernel,
        grid=(n_chunks,),
        in_specs=[ctok] * 6 + [_const_spec(a.shape) for a in scan_in[6:]],
        out_specs=ctok,
        out_shape=jax.ShapeDtypeStruct((B, S, D), F32),
        scratch_shapes=[pltpu.VMEM((B * HEADS // 2, HEAD, 2 * HEAD), F32)],
        compiler_params=pltpu.CompilerParams(dimension_semantics=("arbitrary",),
                                             vmem_limit_bytes=VMEM_LIMIT),
        name="scan",
    )(*scan_in).reshape(T, D)

    ptok = pl.BlockSpec((POST_ROWS, D), lambda i: (i, 0))
    post_in = [xf, row(ln_in_g), row(ln_in_b), ca_t, c1_t, yb_t, w_o[0].astype(BF16),
               row(ln1_g), row(ln1_b), w_up[0].astype(BF16), conv_w[0], row(conv_b),
               w_down[0].astype(BF16), row(ln2_g), row(ln2_b)]
    post_specs = [ptok, _const_spec((1, D)), _const_spec((1, D)), ptok, ptok, ptok] + \
                 [_const_spec(a.shape) for a in post_in[6:]]
    out = pl.pallas_call(
        functools.partial(_post_kernel, S // POST_ROWS),
        grid=(T // POST_ROWS,),
        in_specs=post_specs,
        out_specs=ptok,
        out_shape=tok_shape,
        scratch_shapes=[pltpu.VMEM((8, D_FF), F32)],
        compiler_params=pltpu.CompilerParams(dimension_semantics=("arbitrary",),
                                             vmem_limit_bytes=VMEM_LIMIT),
        name="post",
    )(*post_in)
    return out.reshape(B, S, D)
```

```python
import functools
import math

import jax
import jax.numpy as jnp
from jax import lax
from jax.experimental import pallas as pl
from jax.experimental.pallas import tpu as pltpu

D_MODEL = 1024
SGU_BLOCK = 128
SGU_GROUPS = 8
HEAD = 64
HEADS = D_MODEL // HEAD
DECAY_LORA = 64
ICLR_LORA = 64
GATE_LORA = 160
GATE_LORA_PAD = 256
D_FF = 2688
CHUNK = 64
LN_EPS = 1e-5
GN_EPS = 64e-5
ALPHA = 2.0 ** 0.25
RK_COLS = 3 * D_MODEL + 128 + GATE_LORA_PAD

PRE_ROWS = 256
POST_ROWS = 256
VMEM_LIMIT = 56 * 1024 * 1024

F32 = jnp.float32
BF16 = jnp.bfloat16


def _ln(x, g, b, eps):
    mu = jnp.mean(x, -1, keepdims=True)
    xc = x - mu
    var = jnp.mean(xc * xc, -1, keepdims=True)
    return xc * lax.rsqrt(var + eps) * g + b


def _gelu(x):
    return 0.5 * x * (1.0 + lax.erf(x * (1.0 / math.sqrt(2.0))))


def _sigmoid(x):
    return 0.5 * jnp.tanh(0.5 * x) + 0.5


def _bdot(a, b):
    return jnp.dot(a.astype(BF16), b.astype(BF16), preferred_element_type=F32)


def _head_sum(q, ones_blk):
    qb = q.astype(BF16)
    cols = [jnp.dot(qb[:, g * 128:(g + 1) * 128], ones_blk, preferred_element_type=F32)
            for g in range(q.shape[1] // 128)]
    return jnp.concatenate(cols, axis=1)


def _shift_rows(x, carry, k):
    rolled = pltpu.roll(x, k, 0)
    prev = pltpu.roll(carry, k, 0)
    row = lax.broadcasted_iota(jnp.int32, x.shape, 0)
    head = jnp.concatenate([prev, rolled[8:]], axis=0)
    return jnp.where(row < k, head, rolled)


def _pre_kernel(tiles_per_seq,
                x_ref, lng_ref, lnb_ref, wuv_ref, wrk_ref, wg_ref, bg_ref, mu_ref,
                sg_ref, sb_ref, ws_ref, bst_ref, w0_ref, w2_ref, a0_ref, a2_ref, g2_ref,
                kk_ref, ka_ref, ones_ref,
                r_out, lw_out, k_out, v_out, kk_out, a_out, ca_out, c1_out, hs_out,
                carry_ref):
    i = pl.program_id(0)

    @pl.when(i % tiles_per_seq == 0)
    def _():
        carry_ref[...] = jnp.zeros_like(carry_ref)

    h = _ln(x_ref[...], lng_ref[...], lnb_ref[...], LN_EPS)
    hb = h.astype(BF16)
    hs_out[...] = ALPHA * h

    uv = jnp.dot(hb, wuv_ref[...], preferred_element_type=F32)
    gates_pre = jnp.dot(hb, wg_ref[...], preferred_element_type=F32)
    rk = jnp.dot(hb, wrk_ref[...], preferred_element_type=F32)

    u = _gelu(uv[:, :D_MODEL])
    vn = _ln(_gelu(uv[:, D_MODEL:]), sg_ref[...], sb_ref[...], LN_EPS).astype(BF16)
    rr = lax.broadcasted_iota(jnp.int32, (SGU_BLOCK, SGU_BLOCK), 0) // CHUNK
    cc = lax.broadcasted_iota(jnp.int32, (SGU_BLOCK, SGU_BLOCK), 1) // CHUNK
    blk_mask = rr >= cc
    bst = bst_ref[...]
    z_rows = []
    for nb in range(PRE_ROWS // SGU_BLOCK):
        z_cols = []
        for g in range(SGU_GROUPS):
            wm = jnp.where(blk_mask, ws_ref[g], 0.0).astype(BF16)
            vb = vn[nb * SGU_BLOCK:(nb + 1) * SGU_BLOCK, g * 128:(g + 1) * 128]
            z_cols.append(jnp.dot(wm, vb, preferred_element_type=F32) + bst[:, g:g + 1])
        z_rows.append(jnp.concatenate(z_cols, axis=1))
    ya = u * jnp.concatenate(z_rows, axis=0)

    gates = _sigmoid(gates_pre + bg_ref[...])
    ca_out[...] = gates[:, :D_MODEL] * ya
    gb = gates[:, D_MODEL:]

    rk_prev = _shift_rows(rk, carry_ref[...], 1)
    carry_ref[...] = rk[PRE_ROWS - 8:, :]
    rk = rk + (rk_prev - rk) * mu_ref[...]
    r = rk[:, :D_MODEL]
    k = rk[:, D_MODEL:2 * D_MODEL]
    val = rk[:, 2 * D_MODEL:3 * D_MODEL]
    wdad = rk[:, 3 * D_MODEL:3 * D_MODEL + 128]
    gd = rk[:, 3 * D_MODEL + 128:]

    zw = w0_ref[...] + _bdot(jnp.tanh(wdad), w2_ref[...])
    lw_out[...] = (-math.exp(-0.5)) * _sigmoid(zw)
    a_lr = _sigmoid(a0_ref[...] + _bdot(wdad, a2_ref[...]))
    g = _bdot(_sigmoid(gd), g2_ref[...])
    c1_out[...] = gb * g

    kk = k * kk_ref[...]
    kk_out[...] = kk * lax.rsqrt(jnp.maximum(_head_sum(kk * kk, ones_ref[...]), 1e-24))
    k_out[...] = k * (1.0 + (a_lr - 1.0) * ka_ref[...])
    r_out[...] = r
    v_out[...] = val
    a_out[...] = a_lr


def _dot_nt(a, b):
    return lax.dot_general(a, b, (((1,), (1,)), ((), ())), preferred_element_type=F32)


def _split(x):
    hi = x.astype(BF16)
    return hi, (x - hi.astype(F32)).astype(BF16)


def _scan_kernel(r_ref, lw_ref, k_ref, v_ref, kk_ref, a_ref, rk_ref, gng_ref, gnb_ref,
                 tri2_ref, ones2_ref, y_out, state_ref):
    c = pl.program_id(0)

    @pl.when(c == 0)
    def _():
        state_ref[...] = jnp.zeros_like(state_ref)

    L = CHUNK
    P = 2 * HEAD
    NB = r_ref.shape[0]
    wide = lambda ref: jnp.concatenate([ref[b] for b in range(NB)], axis=1)
    tiled = lambda ref: jnp.concatenate([ref[...]] * NB, axis=1)
    lw = wide(lw_ref)
    lw_hi, lw_lo = _split(lw)
    cs = jnp.dot(tri2_ref[...], jnp.concatenate([lw_hi, lw_lo], axis=0), preferred_element_type=F32)
    e_pos = jnp.exp(cs)
    e_neg = jnp.exp(-cs)
    p_end = e_pos[L - 1:L, :]
    r = wide(r_ref)
    kh = wide(k_ref)
    v = wide(v_ref)
    kk = wide(kk_ref)
    a_t = -kk * jnp.exp(cs - lw)
    r_t = r * e_pos
    b_t = kk * wide(a_ref) * e_neg
    k_t = kh * e_neg
    b_h = b_t * p_end
    k_h = k_t * p_end
    bonus_w = r * kh * tiled(rk_ref)
    gn_g = tiled(gng_ref)
    gn_b = tiled(gnb_ref)

    t_row = lax.broadcasted_iota(jnp.int32, (L, P), 0)
    lane = lax.broadcasted_iota(jnp.int32, (L, P), 1)
    first = lane < HEAD
    s_col = jnp.where(first, lane, lane - HEAD)
    strict = s_col < t_row
    incl = s_col <= t_row
    diag = s_col == t_row
    ones2 = ones2_ref[...]

    def bd(x):
        return jnp.concatenate([jnp.where(first, x, 0), jnp.where(first, 0, x)], axis=0)

    def nat_t(x):
        t = bd(x).T
        return t[:L] + t[L:]

    pairs = range(NB * HEADS // 2)
    sls = [slice(p * P, (p + 1) * P) for p in pairs]
    at_b = [a_t[:, s].astype(BF16) for s in sls]
    rt_b = [r_t[:, s].astype(BF16) for s in sls]
    vbd = [bd(v[:, s].astype(BF16)) for s in sls]
    ar = [jnp.concatenate([at_b[p], rt_b[p]], axis=0) for p in pairs]
    gb = [_dot_nt(ar[p], bd(b_t[:, sls[p]].astype(BF16))) for p in pairs]
    gk = [_dot_nt(ar[p], bd(k_t[:, sls[p]].astype(BF16))) for p in pairs]
    a_ak = [jnp.where(strict, gk[p][:L], 0.0).astype(BF16) for p in pairs]
    a_rb = [jnp.where(incl, gb[p][L:], 0.0).astype(BF16) for p in pairs]
    a_rk = [jnp.where(incl, gk[p][L:], 0.0).astype(BF16) for p in pairs]
    n = [jnp.where(strict, gb[p][:L], 0.0) for p in pairs]
    x = [jnp.where(diag, 1.0, 0.0) + n[p] for p in pairs]
    n = [jnp.dot(n[p].astype(BF16), bd(n[p].astype(BF16)), preferred_element_type=F32) for p in pairs]
    for _ in range(4):
        o = [jnp.dot(jnp.concatenate([n[p], x[p]], axis=0).astype(BF16), bd(n[p].astype(BF16)),
                     preferred_element_type=F32) for p in pairs]
        n = [o[p][:L] for p in pairs]
        x = [x[p] + o[p][L:] for p in pairs]
    x = [(x[p] + jnp.dot(x[p].astype(BF16), bd(n[p].astype(BF16)), preferred_element_type=F32)).astype(BF16)
         for p in pairs]
    zero_b = jnp.zeros((L, P), BF16)
    yh_lhs = [jnp.concatenate(
        [jnp.concatenate([rt_b[p], a_rb[p], a_rk[p]], axis=1),
         jnp.concatenate([zero_b, nat_t(b_h[:, sls[p]]).astype(BF16), nat_t(k_h[:, sls[p]]).astype(BF16)],
                         axis=1)], axis=0) for p in pairs]

    def sums(tiles):
        s = jnp.dot(jnp.concatenate([t.astype(BF16) for t in tiles], axis=0), ones2[:P],
                    preferred_element_type=F32)
        return [s[i * L:(i + 1) * L] for i in range(len(tiles))]

    bonus = sums([bonus_w[:, s] for s in sls])
    pe_hi = p_end.astype(BF16).astype(F32)
    pe_lo = p_end - pe_hi
    p_t = jnp.dot(jnp.concatenate(
        [jnp.concatenate([jnp.where(diag, pe_hi[:, s], 0.0), jnp.where(diag, pe_lo[:, s], 0.0)], axis=1)
         for s in sls], axis=0).astype(BF16), ones2, preferred_element_type=F32)
    p_t = [p_t[i * L:(i + 1) * L] for i in pairs]

    hbd = [bd(state_ref[p].astype(BF16)) for p in pairs]
    u = [jnp.dot(jnp.concatenate([at_b[p], a_ak[p]], axis=1), jnp.concatenate([hbd[p], vbd[p]], axis=0),
                 preferred_element_type=F32) for p in pairs]
    u = [jnp.dot(x[p], bd(u[p].astype(BF16)), preferred_element_type=F32) for p in pairs]
    yh = [jnp.dot(yh_lhs[p], jnp.concatenate([hbd[p], bd(u[p].astype(BF16)), vbd[p]], axis=0),
                  preferred_element_type=F32) for p in pairs]
    for p in pairs:
        state_ref[p] = state_ref[p] * p_t[p] + yh[p][L:]
    y = [yh[p][:L] for p in pairs]
    mu = sums(y)
    yc = [y[p] - mu[p] * (1.0 / HEAD) for p in pairs]
    var = sums([yc[p] * yc[p] for p in pairs])
    var = [var[p] * (1.0 / HEAD) for p in pairs]
    bonus = [bonus[p] * v[:, sls[p]] for p in pairs]
    for p in pairs:
        s = sls[p]
        b, hp = divmod(p, HEADS // 2)
        y_out[b, :, hp * P:(hp + 1) * P] = (
            yc[p] * lax.rsqrt(var[p] + GN_EPS) * gn_g[:, s] + gn_b[:, s] + bonus[p])


def _post_kernel(tiles_per_seq,
                 hs_ref, ca_ref, c1_ref, yb_ref, wo_ref, l1g_ref, l1b_ref,
                 wup_ref, cw_ref, cb_ref, wdn_ref, l2g_ref, l2b_ref,
                 out_ref, carry_ref):
    i = pl.program_id(0)

    @pl.when(i % tiles_per_seq == 0)
    def _():
        carry_ref[...] = jnp.zeros_like(carry_ref)

    mix = ca_ref[...] + c1_ref[...] * yb_ref[...]
    x1 = _ln(hs_ref[...] + _bdot(mix, wo_ref[...]), l1g_ref[...], l1b_ref[...], LN_EPS)
    up = _bdot(x1, wup_ref[...])
    gate = up[:, :D_FF]
    val = up[:, D_FF:]
    carry = carry_ref[...]
    g1 = _shift_rows(gate, carry, 1)
    g2 = _shift_rows(gate, carry, 2)
    carry_ref[...] = gate[POST_ROWS - 8:, :]
    cw = cw_ref[...]
    conv = cb_ref[...] + g2 * cw[0:1, :] + g1 * cw[1:2, :] + gate * cw[2:3, :]
    ffn = _bdot(_gelu(conv) * val, wdn_ref[...])
    out_ref[...] = _ln(ALPHA * x1 + ffn, l2g_ref[...], l2b_ref[...], LN_EPS)


def _const_spec(shape):
    nd = len(shape)
    return pl.BlockSpec(shape, lambda *_: (0,) * nd, pipeline_mode=pl.Buffered(1))


def kernel(x, ln_in_g, ln_in_b, w_in, b_gate, mu_shift, sgu_ln_g, sgu_ln_b, w_s, b_s, w0, w2, a0, a2, g2, k_k, k_a, r_k, lnx_g, lnx_b, w_o, ln1_g, ln1_b, w_up, conv_w, conv_b, w_down, ln2_g, ln2_b):
    B, S, D = x.shape
    assert D == D_MODEL and w_in.shape[0] == 1
    T = B * S
    row = lambda p: p.reshape(1, -1).astype(F32)
    xf = x.reshape(T, D)

    wi = w_in[0]
    c_rk = 2 * D
    c_lora = c_rk + 3 * D
    c_gd = c_lora + DECAY_LORA + ICLR_LORA
    c_gates = c_gd + GATE_LORA
    w_uv = wi[:, :c_rk].astype(BF16)
    w_rk = jnp.concatenate(
        [wi[:, c_rk:c_gates], jnp.zeros((D, GATE_LORA_PAD - GATE_LORA), F32)], axis=1).astype(BF16)
    w_g = wi[:, c_gates:].astype(BF16)
    mu = jnp.concatenate([mu_shift[0], jnp.zeros((GATE_LORA_PAD - GATE_LORA,), F32)]).reshape(1, -1)
    w2p = jnp.concatenate([w2[0], jnp.zeros((ICLR_LORA, D), F32)], axis=0).astype(BF16)
    a2p = jnp.concatenate([jnp.zeros((DECAY_LORA, D), F32), a2[0]], axis=0).astype(BF16)
    g2p = jnp.concatenate([g2[0], jnp.zeros((GATE_LORA_PAD - GATE_LORA, D), F32)], axis=0).astype(BF16)
    lane = jnp.arange(128) // HEAD
    ones_blk = (lane[:, None] == lane[None, :]).astype(BF16)

    tiles_per_seq = S // PRE_ROWS
    tok = pl.BlockSpec((PRE_ROWS, D), lambda i: (i, 0))
    pre_in = [xf, row(ln_in_g), row(ln_in_b), w_uv, w_rk, w_g, row(b_gate), mu,
              row(sgu_ln_g), row(sgu_ln_b), w_s[0], b_s[0].T, row(w0), w2p, row(a0), a2p, g2p,
              row(k_k), row(k_a), ones_blk]
    pre_specs = [tok] + [_const_spec(a.shape) for a in pre_in[1:]]
    tok_shape = jax.ShapeDtypeStruct((T, D), F32)
    r_t, lw_t, k_t, v_t, kk_t, a_t, ca_t, c1_t, hs_t = pl.pallas_call(
        functools.partial(_pre_kernel, tiles_per_seq),
        grid=(T // PRE_ROWS,),
        in_specs=pre_specs,
        out_specs=[tok] * 9,
        out_shape=[tok_shape] * 9,
        scratch_shapes=[pltpu.VMEM((8, RK_COLS), F32)],
        compiler_params=pltpu.CompilerParams(dimension_semantics=("arbitrary",),
                                             vmem_limit_bytes=VMEM_LIMIT),
        name="pre",
    )(*pre_in)

    n_chunks = S // CHUNK
    ctok = pl.BlockSpec((B, CHUNK, D), lambda c: (0, c, 0))
    frame = jnp.arange(CHUNK)
    tri = (frame[:, None] >= frame[None, :]).astype(BF16)
    tri2 = jnp.concatenate([tri, tri], axis=1)
    ones2 = jnp.concatenate([ones_blk, ones_blk], axis=0)
    seq = lambda t: t.reshape(B, S, D)
    scan_in = [seq(r_t), seq(lw_t), seq(k_t), seq(v_t), seq(kk_t), seq(a_t),
               row(r_k), row(lnx_g), row(lnx_b), tri2, ones2]
    yb_t = pl.pallas_call(
        _scan_kernel,
        grid=(n_chunks,),
        in_specs=[ctok] * 6 + [_const_spec(a.shape) for a in scan_in[6:]],
        out_specs=ctok,
        out_shape=jax.ShapeDtypeStruct((B, S, D), F32),
        scratch_shapes=[pltpu.VMEM((B * HEADS // 2, HEAD, 2 * HEAD), F32)],
        compiler_params=pltpu.CompilerParams(dimension_semantics=("arbitrary",),
                                             vmem_limit_bytes=VMEM_LIMIT),
        name="scan",
    )(*scan_in).reshape(T, D)

    ptok = pl.BlockSpec((POST_ROWS, D), lambda i: (i, 0))
    post_in = [hs_t, ca_t, c1_t, yb_t, w_o[0].astype(BF16),
               row(ln1_g), row(ln1_b), w_up[0].astype(BF16), conv_w[0], row(conv_b),
               w_down[0].astype(BF16), row(ln2_g), row(ln2_b)]
    post_specs = [ptok] * 4 + [_const_spec(a.shape) for a in post_in[4:]]
    out = pl.pallas_call(
        functools.partial(_post_kernel, S // POST_ROWS),
        grid=(T // POST_ROWS,),
        in_specs=post_specs,
        out_specs=ptok,
        out_shape=tok_shape,
        scratch_shapes=[pltpu.VMEM((8, D_FF), F32)],
        compiler_params=pltpu.CompilerParams(dimension_semantics=("arbitrary",),
                                             vmem_limit_bytes=VMEM_LIMIT),
        name="post",
    )(*post_in)
    return out.reshape(B, S, D)
```

```python
import functools
import math

import jax
import jax.numpy as jnp
from jax import lax
from jax.experimental import pallas as pl
from jax.experimental.pallas import tpu as pltpu

D_MODEL = 1024
SGU_BLOCK = 128
SGU_GROUPS = 8
HEAD = 64
HEADS = D_MODEL // HEAD
DECAY_LORA = 64
ICLR_LORA = 64
GATE_LORA = 160
GATE_LORA_PAD = 256
D_FF = 2688
CHUNK = 64
LN_EPS = 1e-5
GN_EPS = 64e-5
ALPHA = 2.0 ** 0.25
RK_COLS = 3 * D_MODEL + 128 + GATE_LORA_PAD

PRE_ROWS = 256
POST_ROWS = 256
VMEM_LIMIT = 56 * 1024 * 1024
SCAN_SUB = 2
SCAN_LAG = 4
POST_SUB = 2
POST_LAG = 2

F32 = jnp.float32
BF16 = jnp.bfloat16


def _ln(x, g, b, eps):
    mu = jnp.mean(x, -1, keepdims=True)
    xc = x - mu
    var = jnp.mean(xc * xc, -1, keepdims=True)
    return xc * lax.rsqrt(var + eps) * g + b


def _gelu(x):
    return 0.5 * x * (1.0 + lax.erf(x * (1.0 / math.sqrt(2.0))))


def _sigmoid(x):
    return 0.5 * jnp.tanh(0.5 * x) + 0.5


def _bdot(a, b):
    return jnp.dot(a.astype(BF16), b.astype(BF16), preferred_element_type=F32)


def _head_sum(q, ones_blk):
    qb = q.astype(BF16)
    cols = [jnp.dot(qb[:, g * 128:(g + 1) * 128], ones_blk, preferred_element_type=F32)
            for g in range(q.shape[1] // 128)]
    return jnp.concatenate(cols, axis=1)


def _shift_rows(x, carry, k):
    rolled = pltpu.roll(x, k, 0)
    prev = pltpu.roll(carry, k, 0)
    row = lax.broadcasted_iota(jnp.int32, x.shape, 0)
    head = jnp.concatenate([prev, rolled[8:]], axis=0)
    return jnp.where(row < k, head, rolled)


def _pre_kernel(tiles_per_seq,
                x_ref, lng_ref, lnb_ref, wuv_ref, wrk_ref, wg_ref, bg_ref, mu_ref,
                sg_ref, sb_ref, ws_ref, bst_ref, w0_ref, w2_ref, a0_ref, a2_ref, g2_ref,
                kk_ref, ka_ref, ones_ref,
                r_out, lw_out, k_out, v_out, kk_out, a_out, ca_out, c1_out, hs_out,
                carry_ref):
    i = pl.program_id(0)

    @pl.when(i % tiles_per_seq == 0)
    def _():
        carry_ref[...] = jnp.zeros_like(carry_ref)

    h = _ln(x_ref[...], lng_ref[...], lnb_ref[...], LN_EPS)
    hb = h.astype(BF16)
    hs_out[...] = ALPHA * h

    uv = jnp.dot(hb, wuv_ref[...], preferred_element_type=F32)
    gates_pre = jnp.dot(hb, wg_ref[...], preferred_element_type=F32)
    rk = jnp.dot(hb, wrk_ref[...], preferred_element_type=F32)

    u = _gelu(uv[:, :D_MODEL])
    vn = _ln(_gelu(uv[:, D_MODEL:]), sg_ref[...], sb_ref[...], LN_EPS).astype(BF16)
    rr = lax.broadcasted_iota(jnp.int32, (SGU_BLOCK, SGU_BLOCK), 0) // CHUNK
    cc = lax.broadcasted_iota(jnp.int32, (SGU_BLOCK, SGU_BLOCK), 1) // CHUNK
    blk_mask = rr >= cc
    bst = bst_ref[...]
    z_rows = []
    for nb in range(PRE_ROWS // SGU_BLOCK):
        z_cols = []
        for g in range(SGU_GROUPS):
            wm = jnp.where(blk_mask, ws_ref[g], 0.0).astype(BF16)
            vb = vn[nb * SGU_BLOCK:(nb + 1) * SGU_BLOCK, g * 128:(g + 1) * 128]
            z_cols.append(jnp.dot(wm, vb, preferred_element_type=F32) + bst[:, g:g + 1])
        z_rows.append(jnp.concatenate(z_cols, axis=1))
    ya = u * jnp.concatenate(z_rows, axis=0)

    gates = _sigmoid(gates_pre + bg_ref[...])
    ca_out[...] = gates[:, :D_MODEL] * ya
    gb = gates[:, D_MODEL:]

    rk_prev = _shift_rows(rk, carry_ref[...], 1)
    carry_ref[...] = rk[PRE_ROWS - 8:, :]
    rk = rk + (rk_prev - rk) * mu_ref[...]
    r = rk[:, :D_MODEL]
    k = rk[:, D_MODEL:2 * D_MODEL]
    val = rk[:, 2 * D_MODEL:3 * D_MODEL]
    wdad = rk[:, 3 * D_MODEL:3 * D_MODEL + 128]
    gd = rk[:, 3 * D_MODEL + 128:]

    zw = w0_ref[...] + _bdot(jnp.tanh(wdad), w2_ref[...])
    lw_out[...] = (-math.exp(-0.5)) * _sigmoid(zw)
    a_lr = _sigmoid(a0_ref[...] + _bdot(wdad, a2_ref[...]))
    g = _bdot(_sigmoid(gd), g2_ref[...])
    c1_out[...] = gb * g

    kk = k * kk_ref[...]
    kk_out[...] = kk * lax.rsqrt(jnp.maximum(_head_sum(kk * kk, ones_ref[...]), 1e-24))
    k_out[...] = k * (1.0 + (a_lr - 1.0) * ka_ref[...])
    r_out[...] = r
    v_out[...] = val
    a_out[...] = a_lr


def _dot_nt(a, b):
    return lax.dot_general(a, b, (((1,), (1,)), ((), ())), preferred_element_type=F32)


def _split(x):
    hi = x.astype(BF16)
    return hi, (x - hi.astype(F32)).astype(BF16)


def _interleave(gens, lag):
    gens = list(gens)
    live = []
    step = 0
    while gens or live:
        if gens and step % lag == 0:
            live.append(gens.pop(0))
        for g in list(live):
            if next(g, "done") == "done":
                live.remove(g)
        step += 1


def _scan_kernel(r_ref, lw_ref, k_ref, v_ref, kk_ref, a_ref, rk_ref, gng_ref, gnb_ref,
                 tri2_ref, ones2_ref, y_out, state_ref):
    c = pl.program_id(0)

    @pl.when(c == 0)
    def _():
        state_ref[...] = jnp.zeros_like(state_ref)

    L = CHUNK
    P = 2 * HEAD
    NB = r_ref.shape[0]
    n_sub = r_ref.shape[1] // L

    t_row = lax.broadcasted_iota(jnp.int32, (L, P), 0)
    lane = lax.broadcasted_iota(jnp.int32, (L, P), 1)
    first = lane < HEAD
    s_col = jnp.where(first, lane, lane - HEAD)
    strict = s_col < t_row
    incl = s_col <= t_row
    diag = s_col == t_row
    ones2 = ones2_ref[...]
    zero_b = jnp.zeros((L, P), BF16)
    pairs = range(NB * HEADS // 2)
    sls = [slice(p * P, (p + 1) * P) for p in pairs]

    def bd(x):
        return jnp.concatenate([jnp.where(first, x, 0), jnp.where(first, 0, x)], axis=0)

    def nat_t(x):
        t = bd(x).T
        return t[:L] + t[L:]

    def sums(tiles):
        s = jnp.dot(jnp.concatenate([t.astype(BF16) for t in tiles], axis=0), ones2[:P],
                    preferred_element_type=F32)
        return [s[i * L:(i + 1) * L] for i in range(len(tiles))]

    def chunk(j):
        rows = slice(j * L, (j + 1) * L)
        wide = lambda ref: jnp.concatenate([ref[b, rows, :] for b in range(NB)], axis=1)
        tiled = lambda ref: jnp.concatenate([ref[...]] * NB, axis=1)
        lw = wide(lw_ref)
        lw_hi, lw_lo = _split(lw)
        cs = jnp.dot(tri2_ref[...], jnp.concatenate([lw_hi, lw_lo], axis=0), preferred_element_type=F32)
        e_pos = jnp.exp(cs)
        e_neg = jnp.exp(-cs)
        p_end = e_pos[L - 1:L, :]
        yield
        r = wide(r_ref)
        kh = wide(k_ref)
        v = wide(v_ref)
        kk = wide(kk_ref)
        a_t = -kk * jnp.exp(cs - lw)
        r_t = r * e_pos
        b_t = kk * wide(a_ref) * e_neg
        k_t = kh * e_neg
        at_b = [a_t[:, s].astype(BF16) for s in sls]
        rt_b = [r_t[:, s].astype(BF16) for s in sls]
        bt_bd = [bd(b_t[:, s].astype(BF16)) for s in sls]
        kt_bd = [bd(k_t[:, s].astype(BF16)) for s in sls]
        vbd = [bd(v[:, s].astype(BF16)) for s in sls]
        yield
        b_h = b_t * p_end
        k_h = k_t * p_end
        bk_t = [jnp.concatenate([zero_b, nat_t(b_h[:, s]).astype(BF16), nat_t(k_h[:, s]).astype(BF16)], axis=1)
                for s in sls]
        bonus_w = r * kh * tiled(rk_ref)
        yield
        ar = [jnp.concatenate([at_b[p], rt_b[p]], axis=0) for p in pairs]
        gb = [_dot_nt(ar[p], bt_bd[p]) for p in pairs]
        gk = [_dot_nt(ar[p], kt_bd[p]) for p in pairs]
        a_ak = [jnp.where(strict, gk[p][:L], 0.0).astype(BF16) for p in pairs]
        a_rb = [jnp.where(incl, gb[p][L:], 0.0).astype(BF16) for p in pairs]
        a_rk = [jnp.where(incl, gk[p][L:], 0.0).astype(BF16) for p in pairs]
        n = [jnp.where(strict, gb[p][:L], 0.0) for p in pairs]
        x = [jnp.where(diag, 1.0, 0.0) + n[p] for p in pairs]
        yield
        n = [jnp.dot(n[p].astype(BF16), bd(n[p].astype(BF16)), preferred_element_type=F32) for p in pairs]
        yield
        for _ in range(4):
            o = [jnp.dot(jnp.concatenate([n[p], x[p]], axis=0).astype(BF16), bd(n[p].astype(BF16)),
                         preferred_element_type=F32) for p in pairs]
            n = [o[p][:L] for p in pairs]
            x = [x[p] + o[p][L:] for p in pairs]
            yield
        x = [(x[p] + jnp.dot(x[p].astype(BF16), bd(n[p].astype(BF16)), preferred_element_type=F32)
              ).astype(BF16) for p in pairs]
        yield
        yh_lhs = [jnp.concatenate([jnp.concatenate([rt_b[p], a_rb[p], a_rk[p]], axis=1), bk_t[p]], axis=0)
                  for p in pairs]
        bonus = sums([bonus_w[:, s] for s in sls])
        pe_hi = p_end.astype(BF16).astype(F32)
        pe_lo = p_end - pe_hi
        p_t = jnp.dot(jnp.concatenate(
            [jnp.concatenate([jnp.where(diag, pe_hi[:, s], 0.0), jnp.where(diag, pe_lo[:, s], 0.0)], axis=1)
             for s in sls], axis=0).astype(BF16), ones2, preferred_element_type=F32)
        yield
        hbd = [bd(state_ref[p].astype(BF16)) for p in pairs]
        u = [jnp.dot(jnp.concatenate([at_b[p], a_ak[p]], axis=1), jnp.concatenate([hbd[p], vbd[p]], axis=0),
                     preferred_element_type=F32) for p in pairs]
        yield
        u = [jnp.dot(x[p], bd(u[p].astype(BF16)), preferred_element_type=F32) for p in pairs]
        yield
        yh = [jnp.dot(yh_lhs[p], jnp.concatenate([hbd[p], bd(u[p].astype(BF16)), vbd[p]], axis=0),
                      preferred_element_type=F32) for p in pairs]
        for p in pairs:
            state_ref[p] = state_ref[p] * p_t[p * L:(p + 1) * L] + yh[p][L:]
        yield
        y = [yh[p][:L] for p in pairs]
        mu = sums(y)
        yc = [y[p] - mu[p] * (1.0 / HEAD) for p in pairs]
        yield
        var = sums([yc[p] * yc[p] for p in pairs])
        gn_g = tiled(gng_ref)
        gn_b = tiled(gnb_ref)
        for p in pairs:
            s = sls[p]
            b, hp = divmod(p, HEADS // 2)
            y_out[b, rows, hp * P:(hp + 1) * P] = (
                yc[p] * lax.rsqrt(var[p] * (1.0 / HEAD) + GN_EPS) * gn_g[:, s] + gn_b[:, s]
                + bonus[p] * v[:, s])

    _interleave([chunk(j) for j in range(n_sub)], SCAN_LAG)


def _post_kernel(tiles_per_seq,
                 hs_ref, ca_ref, c1_ref, yb_ref, wo_ref, l1g_ref, l1b_ref,
                 wup_ref, cw_ref, cb_ref, wdn_ref, l2g_ref, l2b_ref,
                 out_ref, carry_ref):
    i = pl.program_id(0)
    n_sub = hs_ref.shape[0] // POST_ROWS

    @pl.when(i % (tiles_per_seq // n_sub) == 0)
    def _():
        carry_ref[...] = jnp.zeros_like(carry_ref)

    def tile(t):
        rows = slice(t * POST_ROWS, (t + 1) * POST_ROWS)
        mix = (ca_ref[rows, :] + c1_ref[rows, :] * yb_ref[rows, :]).astype(BF16)
        yield
        t1 = jnp.dot(mix, wo_ref[...], preferred_element_type=F32)
        yield
        x1 = _ln(hs_ref[rows, :] + t1, l1g_ref[...], l1b_ref[...], LN_EPS)
        x1b = x1.astype(BF16)
        yield
        up = jnp.dot(x1b, wup_ref[...], preferred_element_type=F32)
        yield
        gate = up[:, :D_FF]
        val = up[:, D_FF:]
        carry = carry_ref[...]
        g1 = _shift_rows(gate, carry, 1)
        g2 = _shift_rows(gate, carry, 2)
        carry_ref[...] = gate[POST_ROWS - 8:, :]
        cw = cw_ref[...]
        conv = cb_ref[...] + g2 * cw[0:1, :] + g1 * cw[1:2, :] + gate * cw[2:3, :]
        act = (_gelu(conv) * val).astype(BF16)
        yield
        ffn = jnp.dot(act, wdn_ref[...], preferred_element_type=F32)
        yield
        out_ref[rows, :] = _ln(ALPHA * x1 + ffn, l2g_ref[...], l2b_ref[...], LN_EPS)

    _interleave([tile(t) for t in range(n_sub)], POST_LAG)


def _const_spec(shape):
    nd = len(shape)
    return pl.BlockSpec(shape, lambda *_: (0,) * nd, pipeline_mode=pl.Buffered(1))


def kernel(x, ln_in_g, ln_in_b, w_in, b_gate, mu_shift, sgu_ln_g, sgu_ln_b, w_s, b_s, w0, w2, a0, a2, g2, k_k, k_a, r_k, lnx_g, lnx_b, w_o, ln1_g, ln1_b, w_up, conv_w, conv_b, w_down, ln2_g, ln2_b):
    B, S, D = x.shape
    assert D == D_MODEL and w_in.shape[0] == 1
    T = B * S
    row = lambda p: p.reshape(1, -1).astype(F32)
    xf = x.reshape(T, D)

    wi = w_in[0]
    c_rk = 2 * D
    c_lora = c_rk + 3 * D
    c_gd = c_lora + DECAY_LORA + ICLR_LORA
    c_gates = c_gd + GATE_LORA
    w_uv = wi[:, :c_rk].astype(BF16)
    w_rk = jnp.concatenate(
        [wi[:, c_rk:c_gates], jnp.zeros((D, GATE_LORA_PAD - GATE_LORA), F32)], axis=1).astype(BF16)
    w_g = wi[:, c_gates:].astype(BF16)
    mu = jnp.concatenate([mu_shift[0], jnp.zeros((GATE_LORA_PAD - GATE_LORA,), F32)]).reshape(1, -1)
    w2p = jnp.concatenate([w2[0], jnp.zeros((ICLR_LORA, D), F32)], axis=0).astype(BF16)
    a2p = jnp.concatenate([jnp.zeros((DECAY_LORA, D), F32), a2[0]], axis=0).astype(BF16)
    g2p = jnp.concatenate([g2[0], jnp.zeros((GATE_LORA_PAD - GATE_LORA, D), F32)], axis=0).astype(BF16)
    lane = jnp.arange(128) // HEAD
    ones_blk = (lane[:, None] == lane[None, :]).astype(BF16)

    tiles_per_seq = S // PRE_ROWS
    tok = pl.BlockSpec((PRE_ROWS, D), lambda i: (i, 0))
    pre_in = [xf, row(ln_in_g), row(ln_in_b), w_uv, w_rk, w_g, row(b_gate), mu,
              row(sgu_ln_g), row(sgu_ln_b), w_s[0], b_s[0].T, row(w0), w2p, row(a0), a2p, g2p,
              row(k_k), row(k_a), ones_blk]
    pre_specs = [tok] + [_const_spec(a.shape) for a in pre_in[1:]]
    tok_shape = jax.ShapeDtypeStruct((T, D), F32)
    r_t, lw_t, k_t, v_t, kk_t, a_t, ca_t, c1_t, hs_t = pl.pallas_call(
        functools.partial(_pre_kernel, tiles_per_seq),
        grid=(T // PRE_ROWS,),
        in_specs=pre_specs,
        out_specs=[tok] * 9,
        out_shape=[tok_shape] * 9,
        scratch_shapes=[pltpu.VMEM((8, RK_COLS), F32)],
        compiler_params=pltpu.CompilerParams(dimension_semantics=("arbitrary",),
                                             vmem_limit_bytes=VMEM_LIMIT),
        name="pre",
    )(*pre_in)

    n_chunks = S // CHUNK
    ctok = pl.BlockSpec((B, SCAN_SUB * CHUNK, D), lambda c: (0, c, 0))
    frame = jnp.arange(CHUNK)
    tri = (frame[:, None] >= frame[None, :]).astype(BF16)
    tri2 = jnp.concatenate([tri, tri], axis=1)
    ones2 = jnp.concatenate([ones_blk, ones_blk], axis=0)
    seq = lambda t: t.reshape(B, S, D)
    scan_in = [seq(r_t), seq(lw_t), seq(k_t), seq(v_t), seq(kk_t), seq(a_t),
               row(r_k), row(lnx_g), row(lnx_b), tri2, ones2]
    yb_t = pl.pallas_call(
        _scan_kernel,
        grid=(n_chunks // SCAN_SUB,),
        in_specs=[ctok] * 6 + [_const_spec(a.shape) for a in scan_in[6:]],
        out_specs=ctok,
        out_shape=jax.ShapeDtypeStruct((B, S, D), F32),
        scratch_shapes=[pltpu.VMEM((B * HEADS // 2, HEAD, 2 * HEAD), F32)],
        compiler_params=pltpu.CompilerParams(dimension_semantics=("arbitrary",),
                                             vmem_limit_bytes=VMEM_LIMIT),
        name="scan",
    )(*scan_in).reshape(T, D)

    ptok = pl.BlockSpec((POST_SUB * POST_ROWS, D), lambda i: (i, 0))
    post_in = [hs_t, ca_t, c1_t, yb_t, w_o[0].astype(BF16),
               row(ln1_g), row(ln1_b), w_up[0].astype(BF16), conv_w[0], row(conv_b),
               w_down[0].astype(BF16), row(ln2_g), row(ln2_b)]
    post_specs = [ptok] * 4 + [_const_spec(a.shape) for a in post_in[4:]]
    out = pl.pallas_call(
        functools.partial(_post_kernel, S // POST_ROWS),
        grid=(T // (POST_SUB * POST_ROWS),),
        in_specs=post_specs,
        out_specs=ptok,
        out_shape=tok_shape,
        scratch_shapes=[pltpu.VMEM((8, D_FF), F32)],
        compiler_params=pltpu.CompilerParams(dimension_semantics=("arbitrary",),
                                             vmem_limit_bytes=VMEM_LIMIT),
        name="post",
    )(*post_in)
    return out.reshape(B, S, D)
```

```python
import functools
import math

import jax
import jax.numpy as jnp
from jax import lax
from jax.experimental import pallas as pl
from jax.experimental.pallas import tpu as pltpu

D_MODEL = 1024
SGU_BLOCK = 128
SGU_GROUPS = 8
HEAD = 64
HEADS = D_MODEL // HEAD
DECAY_LORA = 64
ICLR_LORA = 64
GATE_LORA = 160
GATE_LORA_PAD = 256
D_FF = 2688
CHUNK = 64
LN_EPS = 1e-5
GN_EPS = 64e-5
ALPHA = 2.0 ** 0.25
RK_COLS = 3 * D_MODEL + 128 + GATE_LORA_PAD

PRE_ROWS = 256
POST_ROWS = 256
VMEM_LIMIT = 56 * 1024 * 1024
SCAN_SUB = 2
SCAN_LAG = 4
POST_SUB = 2
POST_LAG = 2

F32 = jnp.float32
BF16 = jnp.bfloat16


def _ln(x, g, b, eps):
    mu = jnp.mean(x, -1, keepdims=True)
    xc = x - mu
    var = jnp.mean(xc * xc, -1, keepdims=True)
    return xc * lax.rsqrt(var + eps) * g + b


def _gelu(x):
    return 0.5 * x * (1.0 + lax.erf(x * (1.0 / math.sqrt(2.0))))


def _sigmoid(x):
    return 0.5 * jnp.tanh(0.5 * x) + 0.5


def _bdot(a, b):
    return jnp.dot(a.astype(BF16), b.astype(BF16), preferred_element_type=F32)


def _head_sum(q, ones_blk):
    qb = q.astype(BF16)
    cols = [jnp.dot(qb[:, g * 128:(g + 1) * 128], ones_blk, preferred_element_type=F32)
            for g in range(q.shape[1] // 128)]
    return jnp.concatenate(cols, axis=1)


def _shift_rows(x, carry, k):
    rolled = pltpu.roll(x, k, 0)
    prev = pltpu.roll(carry, k, 0)
    row = lax.broadcasted_iota(jnp.int32, x.shape, 0)
    head = jnp.concatenate([prev, rolled[8:]], axis=0)
    return jnp.where(row < k, head, rolled)


def _pre_kernel(tiles_per_seq,
                x_ref, lng_ref, lnb_ref, wuv_ref, wrk_ref, wg_ref, bg_ref, mu_ref,
                sg_ref, sb_ref, ws_ref, bst_ref, w0_ref, w2_ref, a0_ref, a2_ref, g2_ref,
                kk_ref, ka_ref, ones_ref,
                r_out, lw_out, k_out, v_out, kk_out, a_out, ca_out, c1_out, hs_out,
                carry_ref):
    i = pl.program_id(0)

    @pl.when(i % tiles_per_seq == 0)
    def _():
        carry_ref[...] = jnp.zeros_like(carry_ref)

    h = _ln(x_ref[...], lng_ref[...], lnb_ref[...], LN_EPS)
    hb = h.astype(BF16)
    hs_out[...] = ALPHA * h

    uv = jnp.dot(hb, wuv_ref[...], preferred_element_type=F32)
    gates_pre = jnp.dot(hb, wg_ref[...], preferred_element_type=F32)
    rk = jnp.dot(hb, wrk_ref[...], preferred_element_type=F32)

    u = _gelu(uv[:, :D_MODEL])
    vn = _ln(_gelu(uv[:, D_MODEL:]), sg_ref[...], sb_ref[...], LN_EPS).astype(BF16)
    rr = lax.broadcasted_iota(jnp.int32, (SGU_BLOCK, SGU_BLOCK), 0) // CHUNK
    cc = lax.broadcasted_iota(jnp.int32, (SGU_BLOCK, SGU_BLOCK), 1) // CHUNK
    blk_mask = rr >= cc
    bst = bst_ref[...]
    z_rows = []
    for nb in range(PRE_ROWS // SGU_BLOCK):
        z_cols = []
        for g in range(SGU_GROUPS):
            wm = jnp.where(blk_mask, ws_ref[g], 0.0).astype(BF16)
            vb = vn[nb * SGU_BLOCK:(nb + 1) * SGU_BLOCK, g * 128:(g + 1) * 128]
            z_cols.append(jnp.dot(wm, vb, preferred_element_type=F32) + bst[:, g:g + 1])
        z_rows.append(jnp.concatenate(z_cols, axis=1))
    ya = u * jnp.concatenate(z_rows, axis=0)

    gates = _sigmoid(gates_pre + bg_ref[...])
    ca_out[...] = gates[:, :D_MODEL] * ya
    gb = gates[:, D_MODEL:]

    rk_prev = _shift_rows(rk, carry_ref[...], 1)
    carry_ref[...] = rk[PRE_ROWS - 8:, :]
    rk = rk + (rk_prev - rk) * mu_ref[...]
    r = rk[:, :D_MODEL]
    k = rk[:, D_MODEL:2 * D_MODEL]
    val = rk[:, 2 * D_MODEL:3 * D_MODEL]
    wdad = rk[:, 3 * D_MODEL:3 * D_MODEL + 128]
    gd = rk[:, 3 * D_MODEL + 128:]

    zw = w0_ref[...] + _bdot(jnp.tanh(wdad), w2_ref[...])
    lw_out[...] = (-math.exp(-0.5)) * _sigmoid(zw)
    a_lr = _sigmoid(a0_ref[...] + _bdot(wdad, a2_ref[...]))
    g = _bdot(_sigmoid(gd), g2_ref[...])
    c1_out[...] = gb * g

    kk = k * kk_ref[...]
    kk_out[...] = kk * lax.rsqrt(jnp.maximum(_head_sum(kk * kk, ones_ref[...]), 1e-24))
    k_out[...] = k * (1.0 + (a_lr - 1.0) * ka_ref[...])
    r_out[...] = r
    v_out[...] = val
    a_out[...] = a_lr


def _dot_nt(a, b):
    return lax.dot_general(a, b, (((1,), (1,)), ((), ())), preferred_element_type=F32)


def _split(x):
    hi = x.astype(BF16)
    return hi, (x - hi.astype(F32)).astype(BF16)


def _interleave(gens, lag):
    gens = list(gens)
    live = []
    step = 0
    while gens or live:
        if gens and step % lag == 0:
            live.insert(0, gens.pop(0))
        for g in list(live):
            if next(g, "done") == "done":
                live.remove(g)
        step += 1


def _scan_kernel(r_ref, lw_ref, k_ref, v_ref, kk_ref, a_ref, rk_ref, gng_ref, gnb_ref,
                 tri2_ref, ones2_ref, y_out, state_ref):
    c = pl.program_id(0)

    @pl.when(c == 0)
    def _():
        state_ref[...] = jnp.zeros_like(state_ref)

    L = CHUNK
    P = 2 * HEAD
    NB = r_ref.shape[0]
    n_sub = r_ref.shape[1] // L

    t_row = lax.broadcasted_iota(jnp.int32, (L, P), 0)
    lane = lax.broadcasted_iota(jnp.int32, (L, P), 1)
    first = lane < HEAD
    s_col = jnp.where(first, lane, lane - HEAD)
    strict = s_col < t_row
    incl = s_col <= t_row
    diag = s_col == t_row
    ones2 = ones2_ref[...]
    zero_b = jnp.zeros((L, P), BF16)
    pairs = range(NB * HEADS // 2)
    sls = [slice(p * P, (p + 1) * P) for p in pairs]

    def bd(x):
        return jnp.concatenate([jnp.where(first, x, 0), jnp.where(first, 0, x)], axis=0)

    def nat_t(x):
        t = bd(x).T
        return t[:L] + t[L:]

    def sums(tiles):
        s = jnp.dot(jnp.concatenate([t.astype(BF16) for t in tiles], axis=0), ones2[:P],
                    preferred_element_type=F32)
        return [s[i * L:(i + 1) * L] for i in range(len(tiles))]

    def chunk(j):
        rows = slice(j * L, (j + 1) * L)
        wide = lambda ref: jnp.concatenate([ref[b, rows, :] for b in range(NB)], axis=1)
        tiled = lambda ref: jnp.concatenate([ref[...]] * NB, axis=1)
        lw = wide(lw_ref)
        lw_hi, lw_lo = _split(lw)
        cs = jnp.dot(tri2_ref[...], jnp.concatenate([lw_hi, lw_lo], axis=0), preferred_element_type=F32)
        e_pos = jnp.exp(cs)
        e_neg = jnp.exp(-cs)
        p_end = e_pos[L - 1:L, :]
        yield
        r = wide(r_ref)
        kh = wide(k_ref)
        v = wide(v_ref)
        kk = wide(kk_ref)
        a_t = -kk * jnp.exp(cs - lw)
        r_t = r * e_pos
        b_t = kk * wide(a_ref) * e_neg
        k_t = kh * e_neg
        at_b = [a_t[:, s].astype(BF16) for s in sls]
        rt_b = [r_t[:, s].astype(BF16) for s in sls]
        bt_bd = [bd(b_t[:, s].astype(BF16)) for s in sls]
        kt_bd = [bd(k_t[:, s].astype(BF16)) for s in sls]
        vbd = [bd(v[:, s].astype(BF16)) for s in sls]
        yield
        b_h = b_t * p_end
        k_h = k_t * p_end
        bk_t = [jnp.concatenate([zero_b, nat_t(b_h[:, s]).astype(BF16), nat_t(k_h[:, s]).astype(BF16)], axis=1)
                for s in sls]
        bonus_w = r * kh * tiled(rk_ref)
        yield
        ar = [jnp.concatenate([at_b[p], rt_b[p]], axis=0) for p in pairs]
        gb = [_dot_nt(ar[p], bt_bd[p]) for p in pairs]
        gk = [_dot_nt(ar[p], kt_bd[p]) for p in pairs]
        a_ak = [jnp.where(strict, gk[p][:L], 0.0).astype(BF16) for p in pairs]
        a_rb = [jnp.where(incl, gb[p][L:], 0.0).astype(BF16) for p in pairs]
        a_rk = [jnp.where(incl, gk[p][L:], 0.0).astype(BF16) for p in pairs]
        n = [jnp.where(strict, gb[p][:L], 0.0) for p in pairs]
        x = [jnp.where(diag, 1.0, 0.0) + n[p] for p in pairs]
        yield
        n = [jnp.dot(n[p].astype(BF16), bd(n[p].astype(BF16)), preferred_element_type=F32) for p in pairs]
        yield
        for _ in range(4):
            o = [jnp.dot(jnp.concatenate([n[p], x[p]], axis=0).astype(BF16), bd(n[p].astype(BF16)),
                         preferred_element_type=F32) for p in pairs]
            n = [o[p][:L] for p in pairs]
            x = [x[p] + o[p][L:] for p in pairs]
            yield
        x = [(x[p] + jnp.dot(x[p].astype(BF16), bd(n[p].astype(BF16)), preferred_element_type=F32)
              ).astype(BF16) for p in pairs]
        yield
        yh_lhs = [jnp.concatenate([jnp.concatenate([rt_b[p], a_rb[p], a_rk[p]], axis=1), bk_t[p]], axis=0)
                  for p in pairs]
        bonus = sums([bonus_w[:, s] for s in sls])
        pe_hi = p_end.astype(BF16).astype(F32)
        pe_lo = p_end - pe_hi
        p_t = jnp.dot(jnp.concatenate(
            [jnp.concatenate([jnp.where(diag, pe_hi[:, s], 0.0), jnp.where(diag, pe_lo[:, s], 0.0)], axis=1)
             for s in sls], axis=0).astype(BF16), ones2, preferred_element_type=F32)
        yield
        hbd = [bd(state_ref[p].astype(BF16)) for p in pairs]
        u = [jnp.dot(jnp.concatenate([at_b[p], a_ak[p]], axis=1), jnp.concatenate([hbd[p], vbd[p]], axis=0),
                     preferred_element_type=F32) for p in pairs]
        yield
        u = [jnp.dot(x[p], bd(u[p].astype(BF16)), preferred_element_type=F32) for p in pairs]
        yield
        yh = [jnp.dot(yh_lhs[p], jnp.concatenate([hbd[p], bd(u[p].astype(BF16)), vbd[p]], axis=0),
                      preferred_element_type=F32) for p in pairs]
        for p in pairs:
            state_ref[p] = state_ref[p] * p_t[p * L:(p + 1) * L] + yh[p][L:]
        yield
        y = [yh[p][:L] for p in pairs]
        mu = sums(y)
        yc = [y[p] - mu[p] * (1.0 / HEAD) for p in pairs]
        yield
        var = sums([yc[p] * yc[p] for p in pairs])
        gn_g = tiled(gng_ref)
        gn_b = tiled(gnb_ref)
        for p in pairs:
            s = sls[p]
            b, hp = divmod(p, HEADS // 2)
            y_out[b, rows, hp * P:(hp + 1) * P] = (
                yc[p] * lax.rsqrt(var[p] * (1.0 / HEAD) + GN_EPS) * gn_g[:, s] + gn_b[:, s]
                + bonus[p] * v[:, s])

    _interleave([chunk(j) for j in range(n_sub)], SCAN_LAG)


def _post_kernel(tiles_per_seq,
                 hs_ref, ca_ref, c1_ref, yb_ref, wo_ref, l1g_ref, l1b_ref,
                 wup_ref, cw_ref, cb_ref, wdn_ref, l2g_ref, l2b_ref,
                 out_ref, carry_ref):
    i = pl.program_id(0)
    n_sub = hs_ref.shape[0] // POST_ROWS

    @pl.when(i % (tiles_per_seq // n_sub) == 0)
    def _():
        carry_ref[...] = jnp.zeros_like(carry_ref)

    def tile(t):
        rows = slice(t * POST_ROWS, (t + 1) * POST_ROWS)
        mix = (ca_ref[rows, :] + c1_ref[rows, :] * yb_ref[rows, :]).astype(BF16)
        yield
        t1 = jnp.dot(mix, wo_ref[...], preferred_element_type=F32)
        yield
        x1 = _ln(hs_ref[rows, :] + t1, l1g_ref[...], l1b_ref[...], LN_EPS)
        x1b = x1.astype(BF16)
        yield
        up = jnp.dot(x1b, wup_ref[...], preferred_element_type=F32)
        yield
        gate = up[:, :D_FF]
        val = up[:, D_FF:]
        carry = carry_ref[...]
        g1 = _shift_rows(gate, carry, 1)
        g2 = _shift_rows(gate, carry, 2)
        carry_ref[...] = gate[POST_ROWS - 8:, :]
        cw = cw_ref[...]
        conv = cb_ref[...] + g2 * cw[0:1, :] + g1 * cw[1:2, :] + gate * cw[2:3, :]
        act = (_gelu(conv) * val).astype(BF16)
        yield
        ffn = jnp.dot(act, wdn_ref[...], preferred_element_type=F32)
        yield
        out_ref[rows, :] = _ln(ALPHA * x1 + ffn, l2g_ref[...], l2b_ref[...], LN_EPS)

    _interleave([tile(t) for t in range(n_sub)], POST_LAG)


def _const_spec(shape):
    nd = len(shape)
    return pl.BlockSpec(shape, lambda *_: (0,) * nd, pipeline_mode=pl.Buffered(1))


def kernel(x, ln_in_g, ln_in_b, w_in, b_gate, mu_shift, sgu_ln_g, sgu_ln_b, w_s, b_s, w0, w2, a0, a2, g2, k_k, k_a, r_k, lnx_g, lnx_b, w_o, ln1_g, ln1_b, w_up, conv_w, conv_b, w_down, ln2_g, ln2_b):
    B, S, D = x.shape
    assert D == D_MODEL and w_in.shape[0] == 1
    T = B * S
    row = lambda p: p.reshape(1, -1).astype(F32)
    xf = x.reshape(T, D)

    wi = w_in[0]
    c_rk = 2 * D
    c_lora = c_rk + 3 * D
    c_gd = c_lora + DECAY_LORA + ICLR_LORA
    c_gates = c_gd + GATE_LORA
    w_uv = wi[:, :c_rk].astype(BF16)
    w_rk = jnp.concatenate(
        [wi[:, c_rk:c_gates], jnp.zeros((D, GATE_LORA_PAD - GATE_LORA), F32)], axis=1).astype(BF16)
    w_g = wi[:, c_gates:].astype(BF16)
    mu = jnp.concatenate([mu_shift[0], jnp.zeros((GATE_LORA_PAD - GATE_LORA,), F32)]).reshape(1, -1)
    w2p = jnp.concatenate([w2[0], jnp.zeros((ICLR_LORA, D), F32)], axis=0).astype(BF16)
    a2p = jnp.concatenate([jnp.zeros((DECAY_LORA, D), F32), a2[0]], axis=0).astype(BF16)
    g2p = jnp.concatenate([g2[0], jnp.zeros((GATE_LORA_PAD - GATE_LORA, D), F32)], axis=0).astype(BF16)
    lane = jnp.arange(128) // HEAD
    ones_blk = (lane[:, None] == lane[None, :]).astype(BF16)

    tiles_per_seq = S // PRE_ROWS
    tok = pl.BlockSpec((PRE_ROWS, D), lambda i: (i, 0))
    pre_in = [xf, row(ln_in_g), row(ln_in_b), w_uv, w_rk, w_g, row(b_gate), mu,
              row(sgu_ln_g), row(sgu_ln_b), w_s[0], b_s[0].T, row(w0), w2p, row(a0), a2p, g2p,
              row(k_k), row(k_a), ones_blk]
    pre_specs = [tok] + [_const_spec(a.shape) for a in pre_in[1:]]
    tok_shape = jax.ShapeDtypeStruct((T, D), F32)
    r_t, lw_t, k_t, v_t, kk_t, a_t, ca_t, c1_t, hs_t = pl.pallas_call(
        functools.partial(_pre_kernel, tiles_per_seq),
        grid=(T // PRE_ROWS,),
        in_specs=pre_specs,
        out_specs=[tok] * 9,
        out_shape=[tok_shape] * 9,
        scratch_shapes=[pltpu.VMEM((8, RK_COLS), F32)],
        compiler_params=pltpu.CompilerParams(dimension_semantics=("arbitrary",),
                                             vmem_limit_bytes=VMEM_LIMIT),
        name="pre",
    )(*pre_in)

    n_chunks = S // CHUNK
    ctok = pl.BlockSpec((B, SCAN_SUB * CHUNK, D), lambda c: (0, c, 0))
    frame = jnp.arange(CHUNK)
    tri = (frame[:, None] >= frame[None, :]).astype(BF16)
    tri2 = jnp.concatenate([tri, tri], axis=1)
    ones2 = jnp.concatenate([ones_blk, ones_blk], axis=0)
    seq = lambda t: t.reshape(B, S, D)
    scan_in = [seq(r_t), seq(lw_t), seq(k_t), seq(v_t), seq(kk_t), seq(a_t),
               row(r_k), row(lnx_g), row(lnx_b), tri2, ones2]
    yb_t = pl.pallas_call(
        _scan_kernel,
        grid=(n_chunks // SCAN_SUB,),
        in_specs=[ctok] * 6 + [_const_spec(a.shape) for a in scan_in[6:]],
        out_specs=ctok,
        out_shape=jax.ShapeDtypeStruct((B, S, D), F32),
        scratch_shapes=[pltpu.VMEM((B * HEADS // 2, HEAD, 2 * HEAD), F32)],
        compiler_params=pltpu.CompilerParams(dimension_semantics=("arbitrary",),
                                             vmem_limit_bytes=VMEM_LIMIT),
        name="scan",
    )(*scan_in).reshape(T, D)

    ptok = pl.BlockSpec((POST_SUB * POST_ROWS, D), lambda i: (i, 0))
    post_in = [hs_t, ca_t, c1_t, yb_t, w_o[0].astype(BF16),
               row(ln1_g), row(ln1_b), w_up[0].astype(BF16), conv_w[0], row(conv_b),
               w_down[0].astype(BF16), row(ln2_g), row(ln2_b)]
    post_specs = [ptok] * 4 + [_const_spec(a.shape) for a in post_in[4:]]
    out = pl.pallas_call(
        functools.partial(_post_kernel, S // POST_ROWS),
        grid=(T // (POST_SUB * POST_ROWS),),
        in_specs=post_specs,
        out_specs=ptok,
        out_shape=tok_shape,
        scratch_shapes=[pltpu.VMEM((8, D_FF), F32)],
        compiler_params=pltpu.CompilerParams(dimension_semantics=("arbitrary",),
                                             vmem_limit_bytes=VMEM_LIMIT),
        name="post",
    )(*post_in)
    return out.reshape(B, S, D)
```

```python
import functools
import math

import jax
import jax.numpy as jnp
from jax import lax
from jax.experimental import pallas as pl
from jax.experimental.pallas import tpu as pltpu

D_MODEL = 1024
SGU_BLOCK = 128
SGU_GROUPS = 8
HEAD = 64
HEADS = D_MODEL // HEAD
DECAY_LORA = 64
ICLR_LORA = 64
GATE_LORA = 160
GATE_LORA_PAD = 256
D_FF = 2688
CHUNK = 64
LN_EPS = 1e-5
GN_EPS = 64e-5
ALPHA = 2.0 ** 0.25
RK_COLS = 3 * D_MODEL + 128 + GATE_LORA_PAD

PRE_ROWS = 256
POST_ROWS = 256
VMEM_LIMIT = 56 * 1024 * 1024
SCAN_SUB = 4
SCAN_LAG = 3
POST_SUB = 2
POST_LAG = 2

F32 = jnp.float32
BF16 = jnp.bfloat16


def _ln(x, g, b, eps):
    mu = jnp.mean(x, -1, keepdims=True)
    xc = x - mu
    var = jnp.mean(xc * xc, -1, keepdims=True)
    return xc * lax.rsqrt(var + eps) * g + b


def _gelu(x):
    return 0.5 * x * (1.0 + lax.erf(x * (1.0 / math.sqrt(2.0))))


def _sigmoid(x):
    return 0.5 * jnp.tanh(0.5 * x) + 0.5


def _bdot(a, b):
    return jnp.dot(a.astype(BF16), b.astype(BF16), preferred_element_type=F32)


def _head_sum(q, ones_blk):
    qb = q.astype(BF16)
    cols = [jnp.dot(qb[:, g * 128:(g + 1) * 128], ones_blk, preferred_element_type=F32)
            for g in range(q.shape[1] // 128)]
    return jnp.concatenate(cols, axis=1)


def _shift_rows(x, carry, k):
    rolled = pltpu.roll(x, k, 0)
    prev = pltpu.roll(carry, k, 0)
    row = lax.broadcasted_iota(jnp.int32, x.shape, 0)
    head = jnp.concatenate([prev, rolled[8:]], axis=0)
    return jnp.where(row < k, head, rolled)


def _pre_kernel(tiles_per_seq,
                x_ref, lng_ref, lnb_ref, wuv_ref, wrk_ref, wg_ref, bg_ref, mu_ref,
                sg_ref, sb_ref, ws_ref, bst_ref, w0_ref, w2_ref, a0_ref, a2_ref, g2_ref,
                kk_ref, ka_ref, ones_ref,
                r_out, lw_out, k_out, v_out, kk_out, a_out, ca_out, c1_out, hs_out,
                carry_ref):
    i = pl.program_id(0)

    @pl.when(i % tiles_per_seq == 0)
    def _():
        carry_ref[...] = jnp.zeros_like(carry_ref)

    h = _ln(x_ref[...], lng_ref[...], lnb_ref[...], LN_EPS)
    hb = h.astype(BF16)
    hs_out[...] = ALPHA * h

    uv = jnp.dot(hb, wuv_ref[...], preferred_element_type=F32)
    gates_pre = jnp.dot(hb, wg_ref[...], preferred_element_type=F32)
    rk = jnp.dot(hb, wrk_ref[...], preferred_element_type=F32)

    u = _gelu(uv[:, :D_MODEL])
    vn = _ln(_gelu(uv[:, D_MODEL:]), sg_ref[...], sb_ref[...], LN_EPS).astype(BF16)
    rr = lax.broadcasted_iota(jnp.int32, (SGU_BLOCK, SGU_BLOCK), 0) // CHUNK
    cc = lax.broadcasted_iota(jnp.int32, (SGU_BLOCK, SGU_BLOCK), 1) // CHUNK
    blk_mask = rr >= cc
    bst = bst_ref[...]
    z_rows = []
    for nb in range(PRE_ROWS // SGU_BLOCK):
        z_cols = []
        for g in range(SGU_GROUPS):
            wm = jnp.where(blk_mask, ws_ref[g], 0.0).astype(BF16)
            vb = vn[nb * SGU_BLOCK:(nb + 1) * SGU_BLOCK, g * 128:(g + 1) * 128]
            z_cols.append(jnp.dot(wm, vb, preferred_element_type=F32) + bst[:, g:g + 1])
        z_rows.append(jnp.concatenate(z_cols, axis=1))
    ya = u * jnp.concatenate(z_rows, axis=0)

    gates = _sigmoid(gates_pre + bg_ref[...])
    ca_out[...] = gates[:, :D_MODEL] * ya
    gb = gates[:, D_MODEL:]

    rk_prev = _shift_rows(rk, carry_ref[...], 1)
    carry_ref[...] = rk[PRE_ROWS - 8:, :]
    rk = rk + (rk_prev - rk) * mu_ref[...]
    r = rk[:, :D_MODEL]
    k = rk[:, D_MODEL:2 * D_MODEL]
    val = rk[:, 2 * D_MODEL:3 * D_MODEL]
    wdad = rk[:, 3 * D_MODEL:3 * D_MODEL + 128]
    gd = rk[:, 3 * D_MODEL + 128:]

    zw = w0_ref[...] + _bdot(jnp.tanh(wdad), w2_ref[...])
    lw_out[...] = (-math.exp(-0.5)) * _sigmoid(zw)
    a_lr = _sigmoid(a0_ref[...] + _bdot(wdad, a2_ref[...]))
    g = _bdot(_sigmoid(gd), g2_ref[...])
    c1_out[...] = gb * g

    kk = k * kk_ref[...]
    kk_out[...] = kk * lax.rsqrt(jnp.maximum(_head_sum(kk * kk, ones_ref[...]), 1e-24))
    k_out[...] = k * (1.0 + (a_lr - 1.0) * ka_ref[...])
    r_out[...] = r
    v_out[...] = val
    a_out[...] = a_lr


def _dot_nt(a, b):
    return lax.dot_general(a, b, (((1,), (1,)), ((), ())), preferred_element_type=F32)


def _split(x):
    hi = x.astype(BF16)
    return hi, (x - hi.astype(F32)).astype(BF16)


def _interleave(gens, lag):
    gens = list(gens)
    live = []
    step = 0
    while gens or live:
        if gens and step % lag == 0:
            live.insert(0, gens.pop(0))
        for g in list(live):
            if next(g, "done") == "done":
                live.remove(g)
        step += 1


def _scan_kernel(r_ref, lw_ref, k_ref, v_ref, kk_ref, a_ref, rk_ref, gng_ref, gnb_ref,
                 tri2_ref, ones2_ref, y_out, state_ref):
    c = pl.program_id(0)

    @pl.when(c == 0)
    def _():
        state_ref[...] = jnp.zeros_like(state_ref)

    L = CHUNK
    P = 2 * HEAD
    NB = r_ref.shape[0]
    n_sub = r_ref.shape[1] // L

    t_row = lax.broadcasted_iota(jnp.int32, (L, P), 0)
    lane = lax.broadcasted_iota(jnp.int32, (L, P), 1)
    first = lane < HEAD
    s_col = jnp.where(first, lane, lane - HEAD)
    strict = s_col < t_row
    incl = s_col <= t_row
    diag = s_col == t_row
    ones2 = ones2_ref[...]
    zero_b = jnp.zeros((L, P), BF16)
    pairs = range(NB * HEADS // 2)
    sls = [slice(p * P, (p + 1) * P) for p in pairs]

    def bd(x):
        return jnp.concatenate([jnp.where(first, x, 0), jnp.where(first, 0, x)], axis=0)

    def nat_t(x):
        t = bd(x).T
        return t[:L] + t[L:]

    def sums(tiles):
        s = jnp.dot(jnp.concatenate([t.astype(BF16) for t in tiles], axis=0), ones2[:P],
                    preferred_element_type=F32)
        return [s[i * L:(i + 1) * L] for i in range(len(tiles))]

    def chunk(j):
        rows = slice(j * L, (j + 1) * L)
        wide = lambda ref: jnp.concatenate([ref[b, rows, :] for b in range(NB)], axis=1)
        tiled = lambda ref: jnp.concatenate([ref[...]] * NB, axis=1)
        lw = wide(lw_ref)
        lw_hi, lw_lo = _split(lw)
        cs = jnp.dot(tri2_ref[...], jnp.concatenate([lw_hi, lw_lo], axis=0), preferred_element_type=F32)
        e_pos = jnp.exp(cs)
        e_neg = jnp.exp(-cs)
        p_end = e_pos[L - 1:L, :]
        yield
        r = wide(r_ref)
        kh = wide(k_ref)
        v = wide(v_ref)
        kk = wide(kk_ref)
        a_t = -kk * jnp.exp(cs - lw)
        r_t = r * e_pos
        b_t = kk * wide(a_ref) * e_neg
        k_t = kh * e_neg
        at_b = [a_t[:, s].astype(BF16) for s in sls]
        rt_b = [r_t[:, s].astype(BF16) for s in sls]
        bt_bd = [bd(b_t[:, s].astype(BF16)) for s in sls]
        kt_bd = [bd(k_t[:, s].astype(BF16)) for s in sls]
        vbd = [bd(v[:, s].astype(BF16)) for s in sls]
        yield
        b_h = b_t * p_end
        k_h = k_t * p_end
        bk_t = [jnp.concatenate([zero_b, nat_t(b_h[:, s]).astype(BF16), nat_t(k_h[:, s]).astype(BF16)], axis=1)
                for s in sls]
        bonus_w = r * kh * tiled(rk_ref)
        yield
        ar = [jnp.concatenate([at_b[p], rt_b[p]], axis=0) for p in pairs]
        gb = [_dot_nt(ar[p], bt_bd[p]) for p in pairs]
        gk = [_dot_nt(ar[p], kt_bd[p]) for p in pairs]
        a_ak = [jnp.where(strict, gk[p][:L], 0.0).astype(BF16) for p in pairs]
        a_rb = [jnp.where(incl, gb[p][L:], 0.0).astype(BF16) for p in pairs]
        a_rk = [jnp.where(incl, gk[p][L:], 0.0).astype(BF16) for p in pairs]
        n = [jnp.where(strict, gb[p][:L], 0.0) for p in pairs]
        x = [jnp.where(diag, 1.0, 0.0) + n[p] for p in pairs]
        yield
        n = [jnp.dot(n[p].astype(BF16), bd(n[p].astype(BF16)), preferred_element_type=F32) for p in pairs]
        yield
        for _ in range(4):
            o = [jnp.dot(jnp.concatenate([n[p], x[p]], axis=0).astype(BF16), bd(n[p].astype(BF16)),
                         preferred_element_type=F32) for p in pairs]
            n = [o[p][:L] for p in pairs]
            x = [x[p] + o[p][L:] for p in pairs]
            yield
        x = [(x[p] + jnp.dot(x[p].astype(BF16), bd(n[p].astype(BF16)), preferred_element_type=F32)
              ).astype(BF16) for p in pairs]
        yield
        yh_lhs = [jnp.concatenate([jnp.concatenate([rt_b[p], a_rb[p], a_rk[p]], axis=1), bk_t[p]], axis=0)
                  for p in pairs]
        bonus = sums([bonus_w[:, s] for s in sls])
        pe_hi = p_end.astype(BF16).astype(F32)
        pe_lo = p_end - pe_hi
        p_t = jnp.dot(jnp.concatenate(
            [jnp.concatenate([jnp.where(diag, pe_hi[:, s], 0.0), jnp.where(diag, pe_lo[:, s], 0.0)], axis=1)
             for s in sls], axis=0).astype(BF16), ones2, preferred_element_type=F32)
        yield
        hbd = [bd(state_ref[p].astype(BF16)) for p in pairs]
        u = [jnp.dot(jnp.concatenate([at_b[p], a_ak[p]], axis=1), jnp.concatenate([hbd[p], vbd[p]], axis=0),
                     preferred_element_type=F32) for p in pairs]
        yield
        u = [jnp.dot(x[p], bd(u[p].astype(BF16)), preferred_element_type=F32) for p in pairs]
        yield
        yh = [jnp.dot(yh_lhs[p], jnp.concatenate([hbd[p], bd(u[p].astype(BF16)), vbd[p]], axis=0),
                      preferred_element_type=F32) for p in pairs]
        for p in pairs:
            state_ref[p] = state_ref[p] * p_t[p * L:(p + 1) * L] + yh[p][L:]
        yield
        y = [yh[p][:L] for p in pairs]
        mu = sums(y)
        yc = [y[p] - mu[p] * (1.0 / HEAD) for p in pairs]
        yield
        var = sums([yc[p] * yc[p] for p in pairs])
        gn_g = tiled(gng_ref)
        gn_b = tiled(gnb_ref)
        for p in pairs:
            s = sls[p]
            b, hp = divmod(p, HEADS // 2)
            y_out[b, rows, hp * P:(hp + 1) * P] = (
                yc[p] * lax.rsqrt(var[p] * (1.0 / HEAD) + GN_EPS) * gn_g[:, s] + gn_b[:, s]
                + bonus[p] * v[:, s])

    assert SCAN_LAG >= 3
    _interleave([chunk(j) for j in range(n_sub)], SCAN_LAG)


def _post_kernel(tiles_per_seq,
                 hs_ref, ca_ref, c1_ref, yb_ref, wo_ref, l1g_ref, l1b_ref,
                 wup_ref, cw_ref, cb_ref, wdn_ref, l2g_ref, l2b_ref,
                 out_ref, carry_ref):
    i = pl.program_id(0)
    n_sub = hs_ref.shape[0] // POST_ROWS

    @pl.when(i % (tiles_per_seq // n_sub) == 0)
    def _():
        carry_ref[...] = jnp.zeros_like(carry_ref)

    def tile(t):
        rows = slice(t * POST_ROWS, (t + 1) * POST_ROWS)
        mix = (ca_ref[rows, :] + c1_ref[rows, :] * yb_ref[rows, :]).astype(BF16)
        yield
        t1 = jnp.dot(mix, wo_ref[...], preferred_element_type=F32)
        yield
        x1 = _ln(hs_ref[rows, :] + t1, l1g_ref[...], l1b_ref[...], LN_EPS)
        x1b = x1.astype(BF16)
        yield
        up = jnp.dot(x1b, wup_ref[...], preferred_element_type=F32)
        yield
        gate = up[:, :D_FF]
        val = up[:, D_FF:]
        carry = carry_ref[...]
        g1 = _shift_rows(gate, carry, 1)
        g2 = _shift_rows(gate, carry, 2)
        carry_ref[...] = gate[POST_ROWS - 8:, :]
        cw = cw_ref[...]
        conv = cb_ref[...] + g2 * cw[0:1, :] + g1 * cw[1:2, :] + gate * cw[2:3, :]
        act = (_gelu(conv) * val).astype(BF16)
        yield
        ffn = jnp.dot(act, wdn_ref[...], preferred_element_type=F32)
        yield
        out_ref[rows, :] = _ln(ALPHA * x1 + ffn, l2g_ref[...], l2b_ref[...], LN_EPS)

    _interleave([tile(t) for t in range(n_sub)], POST_LAG)


def _const_spec(shape):
    nd = len(shape)
    return pl.BlockSpec(shape, lambda *_: (0,) * nd, pipeline_mode=pl.Buffered(1))


def kernel(x, ln_in_g, ln_in_b, w_in, b_gate, mu_shift, sgu_ln_g, sgu_ln_b, w_s, b_s, w0, w2, a0, a2, g2, k_k, k_a, r_k, lnx_g, lnx_b, w_o, ln1_g, ln1_b, w_up, conv_w, conv_b, w_down, ln2_g, ln2_b):
    B, S, D = x.shape
    assert D == D_MODEL and w_in.shape[0] == 1
    T = B * S
    row = lambda p: p.reshape(1, -1).astype(F32)
    xf = x.reshape(T, D)

    wi = w_in[0]
    c_rk = 2 * D
    c_lora = c_rk + 3 * D
    c_gd = c_lora + DECAY_LORA + ICLR_LORA
    c_gates = c_gd + GATE_LORA
    w_uv = wi[:, :c_rk].astype(BF16)
    w_rk = jnp.concatenate(
        [wi[:, c_rk:c_gates], jnp.zeros((D, GATE_LORA_PAD - GATE_LORA), F32)], axis=1).astype(BF16)
    w_g = wi[:, c_gates:].astype(BF16)
    mu = jnp.concatenate([mu_shift[0], jnp.zeros((GATE_LORA_PAD - GATE_LORA,), F32)]).reshape(1, -1)
    w2p = jnp.concatenate([w2[0], jnp.zeros((ICLR_LORA, D), F32)], axis=0).astype(BF16)
    a2p = jnp.concatenate([jnp.zeros((DECAY_LORA, D), F32), a2[0]], axis=0).astype(BF16)
    g2p = jnp.concatenate([g2[0], jnp.zeros((GATE_LORA_PAD - GATE_LORA, D), F32)], axis=0).astype(BF16)
    lane = jnp.arange(128) // HEAD
    ones_blk = (lane[:, None] == lane[None, :]).astype(BF16)

    tiles_per_seq = S // PRE_ROWS
    tok = pl.BlockSpec((PRE_ROWS, D), lambda i: (i, 0))
    pre_in = [xf, row(ln_in_g), row(ln_in_b), w_uv, w_rk, w_g, row(b_gate), mu,
              row(sgu_ln_g), row(sgu_ln_b), w_s[0], b_s[0].T, row(w0), w2p, row(a0), a2p, g2p,
              row(k_k), row(k_a), ones_blk]
    pre_specs = [tok] + [_const_spec(a.shape) for a in pre_in[1:]]
    tok_shape = jax.ShapeDtypeStruct((T, D), F32)
    r_t, lw_t, k_t, v_t, kk_t, a_t, ca_t, c1_t, hs_t = pl.pallas_call(
        functools.partial(_pre_kernel, tiles_per_seq),
        grid=(T // PRE_ROWS,),
        in_specs=pre_specs,
        out_specs=[tok] * 9,
        out_shape=[tok_shape] * 9,
        scratch_shapes=[pltpu.VMEM((8, RK_COLS), F32)],
        compiler_params=pltpu.CompilerParams(dimension_semantics=("arbitrary",),
                                             vmem_limit_bytes=VMEM_LIMIT),
        name="pre",
    )(*pre_in)

    n_chunks = S // CHUNK
    ctok = pl.BlockSpec((B, SCAN_SUB * CHUNK, D), lambda c: (0, c, 0))
    frame = jnp.arange(CHUNK)
    tri = (frame[:, None] >= frame[None, :]).astype(BF16)
    tri2 = jnp.concatenate([tri, tri], axis=1)
    ones2 = jnp.concatenate([ones_blk, ones_blk], axis=0)
    seq = lambda t: t.reshape(B, S, D)
    scan_in = [seq(r_t), seq(lw_t), seq(k_t), seq(v_t), seq(kk_t), seq(a_t),
               row(r_k), row(lnx_g), row(lnx_b), tri2, ones2]
    yb_t = pl.pallas_call(
        _scan_kernel,
        grid=(n_chunks // SCAN_SUB,),
        in_specs=[ctok] * 6 + [_const_spec(a.shape) for a in scan_in[6:]],
        out_specs=ctok,
        out_shape=jax.ShapeDtypeStruct((B, S, D), F32),
        scratch_shapes=[pltpu.VMEM((B * HEADS // 2, HEAD, 2 * HEAD), F32)],
        compiler_params=pltpu.CompilerParams(dimension_semantics=("arbitrary",),
                                             vmem_limit_bytes=VMEM_LIMIT),
        name="scan",
    )(*scan_in).reshape(T, D)

    ptok = pl.BlockSpec((POST_SUB * POST_ROWS, D), lambda i: (i, 0))
    post_in = [hs_t, ca_t, c1_t, yb_t, w_o[0].astype(BF16),
               row(ln1_g), row(ln1_b), w_up[0].astype(BF16), conv_w[0], row(conv_b),
               w_down[0].astype(BF16), row(ln2_g), row(ln2_b)]
    post_specs = [ptok] * 4 + [_const_spec(a.shape) for a in post_in[4:]]
    out = pl.pallas_call(
        functools.partial(_post_kernel, S // POST_ROWS),
        grid=(T // (POST_SUB * POST_ROWS),),
        in_specs=post_specs,
        out_specs=ptok,
        out_shape=tok_shape,
        scratch_shapes=[pltpu.VMEM((8, D_FF), F32)],
        compiler_params=pltpu.CompilerParams(dimension_semantics=("arbitrary",),
                                             vmem_limit_bytes=VMEM_LIMIT),
        name="post",
    )(*post_in)
    return out.reshape(B, S, D)
```

```python
import functools
import math

import jax
import jax.numpy as jnp
from jax import lax
from jax.experimental import pallas as pl
from jax.experimental.pallas import tpu as pltpu

D_MODEL = 1024
SGU_BLOCK = 128
SGU_GROUPS = 8
HEAD = 64
HEADS = D_MODEL // HEAD
DECAY_LORA = 64
ICLR_LORA = 64
GATE_LORA = 160
GATE_LORA_PAD = 256
D_FF = 2688
CHUNK = 64
LN_EPS = 1e-5
GN_EPS = 64e-5
ALPHA = 2.0 ** 0.25
LANES = 128
LORA_PAIR = DECAY_LORA + ICLR_LORA
assert LORA_PAIR == LANES and SGU_BLOCK == LANES and 2 * HEAD == LANES
RK_COLS = 3 * D_MODEL + LORA_PAIR + GATE_LORA_PAD

PRE_ROWS = 256
POST_ROWS = 256
V7X_VMEM_BYTES = 64 * 1024 * 1024
VMEM_LIMIT = V7X_VMEM_BYTES * 7 // 8
SCAN_SUB = 4
SCAN_LAG = 3
POST_SUB = 2
POST_LAG = 2

F32 = jnp.float32
BF16 = jnp.bfloat16


def _ln(x, g, b, eps):
    mu = jnp.mean(x, -1, keepdims=True)
    xc = x - mu
    var = jnp.mean(xc * xc, -1, keepdims=True)
    return xc * lax.rsqrt(var + eps) * g + b


def _gelu(x):
    return 0.5 * x * (1.0 + lax.erf(x * (1.0 / math.sqrt(2.0))))


def _sigmoid(x):
    return 0.5 * jnp.tanh(0.5 * x) + 0.5


def _bdot(a, b):
    return jnp.dot(a.astype(BF16), b.astype(BF16), preferred_element_type=F32)


def _head_sum(q, ones_blk):
    qb = q.astype(BF16)
    cols = [jnp.dot(qb[:, g * LANES:(g + 1) * LANES], ones_blk, preferred_element_type=F32)
            for g in range(q.shape[1] // LANES)]
    return jnp.concatenate(cols, axis=1)


def _shift_rows(x, carry, k):
    rolled = pltpu.roll(x, k, 0)
    prev = pltpu.roll(carry, k, 0)
    row = lax.broadcasted_iota(jnp.int32, x.shape, 0)
    head = jnp.concatenate([prev, rolled[8:]], axis=0)
    return jnp.where(row < k, head, rolled)


def _pre_kernel(tiles_per_seq,
                x_ref, lng_ref, lnb_ref, wuv_ref, wrk_ref, wg_ref, bg_ref, mu_ref,
                sg_ref, sb_ref, ws_ref, bst_ref, w0_ref, w2_ref, a0_ref, a2_ref, g2_ref,
                kk_ref, ka_ref, ones_ref,
                r_out, lw_out, k_out, v_out, kk_out, a_out, ca_out, c1_out, hs_out,
                carry_ref):
    i = pl.program_id(0)

    @pl.when(i % tiles_per_seq == 0)
    def _():
        carry_ref[...] = jnp.zeros_like(carry_ref)

    h = _ln(x_ref[...], lng_ref[...], lnb_ref[...], LN_EPS)
    hb = h.astype(BF16)
    hs_out[...] = ALPHA * h

    uv = jnp.dot(hb, wuv_ref[...], preferred_element_type=F32)
    gates_pre = jnp.dot(hb, wg_ref[...], preferred_element_type=F32)
    rk = jnp.dot(hb, wrk_ref[...], preferred_element_type=F32)

    u = _gelu(uv[:, :D_MODEL])
    vn = _ln(_gelu(uv[:, D_MODEL:]), sg_ref[...], sb_ref[...], LN_EPS).astype(BF16)
    rr = lax.broadcasted_iota(jnp.int32, (SGU_BLOCK, SGU_BLOCK), 0) // CHUNK
    cc = lax.broadcasted_iota(jnp.int32, (SGU_BLOCK, SGU_BLOCK), 1) // CHUNK
    blk_mask = rr >= cc
    bst = bst_ref[...]
    z_rows = []
    for nb in range(PRE_ROWS // SGU_BLOCK):
        z_cols = []
        for g in range(SGU_GROUPS):
            wm = jnp.where(blk_mask, ws_ref[g], 0.0).astype(BF16)
            vb = vn[nb * SGU_BLOCK:(nb + 1) * SGU_BLOCK, g * LANES:(g + 1) * LANES]
            z_cols.append(jnp.dot(wm, vb, preferred_element_type=F32) + bst[:, g:g + 1])
        z_rows.append(jnp.concatenate(z_cols, axis=1))
    ya = u * jnp.concatenate(z_rows, axis=0)

    gates = _sigmoid(gates_pre + bg_ref[...])
    ca_out[...] = gates[:, :D_MODEL] * ya
    gb = gates[:, D_MODEL:]

    rk_prev = _shift_rows(rk, carry_ref[...], 1)
    carry_ref[...] = rk[PRE_ROWS - 8:, :]
    rk = rk + (rk_prev - rk) * mu_ref[...]
    r = rk[:, :D_MODEL]
    k = rk[:, D_MODEL:2 * D_MODEL]
    val = rk[:, 2 * D_MODEL:3 * D_MODEL]
    wdad = rk[:, 3 * D_MODEL:3 * D_MODEL + LORA_PAIR]
    gd = rk[:, 3 * D_MODEL + LORA_PAIR:]

    zw = w0_ref[...] + _bdot(jnp.tanh(wdad), w2_ref[...])
    lw_out[...] = (-math.exp(-0.5)) * _sigmoid(zw)
    a_lr = _sigmoid(a0_ref[...] + _bdot(wdad, a2_ref[...]))
    g = _bdot(_sigmoid(gd), g2_ref[...])
    c1_out[...] = gb * g

    kk = k * kk_ref[...]
    kk_out[...] = kk * lax.rsqrt(jnp.maximum(_head_sum(kk * kk, ones_ref[...]), 1e-24))
    k_out[...] = k * (1.0 + (a_lr - 1.0) * ka_ref[...])
    r_out[...] = r
    v_out[...] = val
    a_out[...] = a_lr


def _dot_nt(a, b):
    return lax.dot_general(a, b, (((1,), (1,)), ((), ())), preferred_element_type=F32)


def _split(x):
    hi = x.astype(BF16)
    return hi, (x - hi.astype(F32)).astype(BF16)


def _interleave(gens, lag):
    gens = list(gens)
    live = []
    step = 0
    while gens or live:
        if gens and step % lag == 0:
            live.insert(0, gens.pop(0))
        for g in list(live):
            if next(g, "done") == "done":
                live.remove(g)
        step += 1


def _scan_kernel(r_ref, lw_ref, k_ref, v_ref, kk_ref, a_ref, rk_ref, gng_ref, gnb_ref,
                 tri2_ref, ones2_ref, y_out, state_ref):
    c = pl.program_id(0)

    @pl.when(c == 0)
    def _():
        state_ref[...] = jnp.zeros_like(state_ref)

    L = CHUNK
    P = 2 * HEAD
    NB = r_ref.shape[0]
    n_sub = r_ref.shape[1] // L

    t_row = lax.broadcasted_iota(jnp.int32, (L, P), 0)
    lane = lax.broadcasted_iota(jnp.int32, (L, P), 1)
    first = lane < HEAD
    s_col = jnp.where(first, lane, lane - HEAD)
    strict = s_col < t_row
    incl = s_col <= t_row
    diag = s_col == t_row
    ones2 = ones2_ref[...]
    zero_b = jnp.zeros((L, P), BF16)
    pairs = range(NB * HEADS // 2)
    sls = [slice(p * P, (p + 1) * P) for p in pairs]

    def bd(x):
        return jnp.concatenate([jnp.where(first, x, 0), jnp.where(first, 0, x)], axis=0)

    def nat_t(x):
        t = bd(x).T
        return t[:L] + t[L:]

    def sums(tiles):
        s = jnp.dot(jnp.concatenate([t.astype(BF16) for t in tiles], axis=0), ones2[:P],
                    preferred_element_type=F32)
        return [s[i * L:(i + 1) * L] for i in range(len(tiles))]

    def chunk(j):
        rows = slice(j * L, (j + 1) * L)
        wide = lambda ref: jnp.concatenate([ref[b, rows, :] for b in range(NB)], axis=1)
        tiled = lambda ref: jnp.concatenate([ref[...]] * NB, axis=1)
        lw = wide(lw_ref)
        lw_hi, lw_lo = _split(lw)
        cs = jnp.dot(tri2_ref[...], jnp.concatenate([lw_hi, lw_lo], axis=0), preferred_element_type=F32)
        e_pos = jnp.exp(cs)
        e_neg = jnp.exp(-cs)
        p_end = e_pos[L - 1:L, :]
        yield
        r = wide(r_ref)
        kh = wide(k_ref)
        v = wide(v_ref)
        kk = wide(kk_ref)
        a_t = -kk * jnp.exp(cs - lw)
        r_t = r * e_pos
        b_t = kk * wide(a_ref) * e_neg
        k_t = kh * e_neg
        at_b = [a_t[:, s].astype(BF16) for s in sls]
        rt_b = [r_t[:, s].astype(BF16) for s in sls]
        bt_bd = [bd(b_t[:, s].astype(BF16)) for s in sls]
        kt_bd = [bd(k_t[:, s].astype(BF16)) for s in sls]
        vbd = [bd(v[:, s].astype(BF16)) for s in sls]
        yield
        b_h = b_t * p_end
        k_h = k_t * p_end
        bk_t = [jnp.concatenate([zero_b, nat_t(b_h[:, s]).astype(BF16), nat_t(k_h[:, s]).astype(BF16)], axis=1)
                for s in sls]
        bonus_w = r * kh * tiled(rk_ref)
        yield
        ar = [jnp.concatenate([at_b[p], rt_b[p]], axis=0) for p in pairs]
        gb = [_dot_nt(ar[p], bt_bd[p]) for p in pairs]
        gk = [_dot_nt(ar[p], kt_bd[p]) for p in pairs]
        a_ak = [jnp.where(strict, gk[p][:L], 0.0).astype(BF16) for p in pairs]
        a_rb = [jnp.where(incl, gb[p][L:], 0.0).astype(BF16) for p in pairs]
        a_rk = [jnp.where(incl, gk[p][L:], 0.0).astype(BF16) for p in pairs]
        n = [jnp.where(strict, gb[p][:L], 0.0) for p in pairs]
        x = [jnp.where(diag, 1.0, 0.0) + n[p] for p in pairs]
        yield
        n = [jnp.dot(n[p].astype(BF16), bd(n[p].astype(BF16)), preferred_element_type=F32) for p in pairs]
        yield
        for _ in range(4):
            o = [jnp.dot(jnp.concatenate([n[p], x[p]], axis=0).astype(BF16), bd(n[p].astype(BF16)),
                         preferred_element_type=F32) for p in pairs]
            n = [o[p][:L] for p in pairs]
            x = [x[p] + o[p][L:] for p in pairs]
            yield
        x = [(x[p] + jnp.dot(x[p].astype(BF16), bd(n[p].astype(BF16)), preferred_element_type=F32)
              ).astype(BF16) for p in pairs]
        yield
        yh_lhs = [jnp.concatenate([jnp.concatenate([rt_b[p], a_rb[p], a_rk[p]], axis=1), bk_t[p]], axis=0)
                  for p in pairs]
        bonus = sums([bonus_w[:, s] for s in sls])
        pe_hi = p_end.astype(BF16).astype(F32)
        pe_lo = p_end - pe_hi
        p_t = jnp.dot(jnp.concatenate(
            [jnp.concatenate([jnp.where(diag, pe_hi[:, s], 0.0), jnp.where(diag, pe_lo[:, s], 0.0)], axis=1)
             for s in sls], axis=0).astype(BF16), ones2, preferred_element_type=F32)
        yield
        hbd = [bd(state_ref[p].astype(BF16)) for p in pairs]
        u = [jnp.dot(jnp.concatenate([at_b[p], a_ak[p]], axis=1), jnp.concatenate([hbd[p], vbd[p]], axis=0),
                     preferred_element_type=F32) for p in pairs]
        yield
        u = [jnp.dot(x[p], bd(u[p].astype(BF16)), preferred_element_type=F32) for p in pairs]
        yield
        yh = [jnp.dot(yh_lhs[p], jnp.concatenate([hbd[p], bd(u[p].astype(BF16)), vbd[p]], axis=0),
                      preferred_element_type=F32) for p in pairs]
        for p in pairs:
            state_ref[p] = state_ref[p] * p_t[p * L:(p + 1) * L] + yh[p][L:]
        yield
        y = [yh[p][:L] for p in pairs]
        mu = sums(y)
        yc = [y[p] - mu[p] * (1.0 / HEAD) for p in pairs]
        yield
        var = sums([yc[p] * yc[p] for p in pairs])
        gn_g = tiled(gng_ref)
        gn_b = tiled(gnb_ref)
        for p in pairs:
            s = sls[p]
            b, hp = divmod(p, HEADS // 2)
            y_out[b, rows, hp * P:(hp + 1) * P] = (
                yc[p] * lax.rsqrt(var[p] * (1.0 / HEAD) + GN_EPS) * gn_g[:, s] + gn_b[:, s]
                + bonus[p] * v[:, s])

    assert SCAN_LAG >= 3
    _interleave([chunk(j) for j in range(n_sub)], SCAN_LAG)


def _post_kernel(tiles_per_seq,
                 hs_ref, ca_ref, c1_ref, yb_ref, wo_ref, l1g_ref, l1b_ref,
                 wup_ref, cw_ref, cb_ref, wdn_ref, l2g_ref, l2b_ref,
                 out_ref, carry_ref):
    i = pl.program_id(0)
    n_sub = hs_ref.shape[0] // POST_ROWS

    @pl.when(i % (tiles_per_seq // n_sub) == 0)
    def _():
        carry_ref[...] = jnp.zeros_like(carry_ref)

    def tile(t):
        rows = slice(t * POST_ROWS, (t + 1) * POST_ROWS)
        mix = (ca_ref[rows, :] + c1_ref[rows, :] * yb_ref[rows, :]).astype(BF16)
        yield
        t1 = jnp.dot(mix, wo_ref[...], preferred_element_type=F32)
        yield
        x1 = _ln(hs_ref[rows, :] + t1, l1g_ref[...], l1b_ref[...], LN_EPS)
        x1b = x1.astype(BF16)
        yield
        up = jnp.dot(x1b, wup_ref[...], preferred_element_type=F32)
        yield
        gate = up[:, :D_FF]
        val = up[:, D_FF:]
        carry = carry_ref[...]
        g1 = _shift_rows(gate, carry, 1)
        g2 = _shift_rows(gate, carry, 2)
        carry_ref[...] = gate[POST_ROWS - 8:, :]
        cw = cw_ref[...]
        conv = cb_ref[...] + g2 * cw[0:1, :] + g1 * cw[1:2, :] + gate * cw[2:3, :]
        act = (_gelu(conv) * val).astype(BF16)
        yield
        ffn = jnp.dot(act, wdn_ref[...], preferred_element_type=F32)
        yield
        out_ref[rows, :] = _ln(ALPHA * x1 + ffn, l2g_ref[...], l2b_ref[...], LN_EPS)

    _interleave([tile(t) for t in range(n_sub)], POST_LAG)


def _const_spec(shape):
    nd = len(shape)
    return pl.BlockSpec(shape, lambda *_: (0,) * nd, pipeline_mode=pl.Buffered(1))


def kernel(x, ln_in_g, ln_in_b, w_in, b_gate, mu_shift, sgu_ln_g, sgu_ln_b, w_s, b_s, w0, w2, a0, a2, g2, k_k, k_a, r_k, lnx_g, lnx_b, w_o, ln1_g, ln1_b, w_up, conv_w, conv_b, w_down, ln2_g, ln2_b):
    B, S, D = x.shape
    assert D == D_MODEL and w_in.shape[0] == 1
    T = B * S
    row = lambda p: p.reshape(1, -1).astype(F32)
    xf = x.reshape(T, D)

    wi = w_in[0]
    c_rk = 2 * D
    c_lora = c_rk + 3 * D
    c_gd = c_lora + DECAY_LORA + ICLR_LORA
    c_gates = c_gd + GATE_LORA
    w_uv = wi[:, :c_rk].astype(BF16)
    w_rk = jnp.concatenate(
        [wi[:, c_rk:c_gates], jnp.zeros((D, GATE_LORA_PAD - GATE_LORA), F32)], axis=1).astype(BF16)
    w_g = wi[:, c_gates:].astype(BF16)
    mu = jnp.concatenate([mu_shift[0], jnp.zeros((GATE_LORA_PAD - GATE_LORA,), F32)]).reshape(1, -1)
    w2p = jnp.concatenate([w2[0], jnp.zeros((ICLR_LORA, D), F32)], axis=0).astype(BF16)
    a2p = jnp.concatenate([jnp.zeros((DECAY_LORA, D), F32), a2[0]], axis=0).astype(BF16)
    g2p = jnp.concatenate([g2[0], jnp.zeros((GATE_LORA_PAD - GATE_LORA, D), F32)], axis=0).astype(BF16)
    lane = jnp.arange(LANES) // HEAD
    ones_blk = (lane[:, None] == lane[None, :]).astype(BF16)

    tiles_per_seq = S // PRE_ROWS
    tok = pl.BlockSpec((PRE_ROWS, D), lambda i: (i, 0))
    pre_in = [xf, row(ln_in_g), row(ln_in_b), w_uv, w_rk, w_g, row(b_gate), mu,
              row(sgu_ln_g), row(sgu_ln_b), w_s[0], b_s[0].T, row(w0), w2p, row(a0), a2p, g2p,
              row(k_k), row(k_a), ones_blk]
    pre_specs = [tok] + [_const_spec(a.shape) for a in pre_in[1:]]
    tok_shape = jax.ShapeDtypeStruct((T, D), F32)
    r_t, lw_t, k_t, v_t, kk_t, a_t, ca_t, c1_t, hs_t = pl.pallas_call(
        functools.partial(_pre_kernel, tiles_per_seq),
        grid=(T // PRE_ROWS,),
        in_specs=pre_specs,
        out_specs=[tok] * 9,
        out_shape=[tok_shape] * 9,
        scratch_shapes=[pltpu.VMEM((8, RK_COLS), F32)],
        compiler_params=pltpu.CompilerParams(dimension_semantics=("arbitrary",),
                                             vmem_limit_bytes=VMEM_LIMIT),
        name="pre",
    )(*pre_in)

    n_chunks = S // CHUNK
    ctok = pl.BlockSpec((B, SCAN_SUB * CHUNK, D), lambda c: (0, c, 0))
    frame = jnp.arange(CHUNK)
    tri = (frame[:, None] >= frame[None, :]).astype(BF16)
    tri2 = jnp.concatenate([tri, tri], axis=1)
    ones2 = jnp.concatenate([ones_blk, ones_blk], axis=0)
    seq = lambda t: t.reshape(B, S, D)
    scan_in = [seq(r_t), seq(lw_t), seq(k_t), seq(v_t), seq(kk_t), seq(a_t),
               row(r_k), row(lnx_g), row(lnx_b), tri2, ones2]
    yb_t = pl.pallas_call(
        _scan_kernel,
        grid=(n_chunks // SCAN_SUB,),
        in_specs=[ctok] * 6 + [_const_spec(a.shape) for a in scan_in[6:]],
        out_specs=ctok,
        out_shape=jax.ShapeDtypeStruct((B, S, D), F32),
        scratch_shapes=[pltpu.VMEM((B * HEADS // 2, HEAD, 2 * HEAD), F32)],
        compiler_params=pltpu.CompilerParams(dimension_semantics=("arbitrary",),
                                             vmem_limit_bytes=VMEM_LIMIT),
        name="scan",
    )(*scan_in).reshape(T, D)

    ptok = pl.BlockSpec((POST_SUB * POST_ROWS, D), lambda i: (i, 0))
    post_in = [hs_t, ca_t, c1_t, yb_t, w_o[0].astype(BF16),
               row(ln1_g), row(ln1_b), w_up[0].astype(BF16), conv_w[0], row(conv_b),
               w_down[0].astype(BF16), row(ln2_g), row(ln2_b)]
    post_specs = [ptok] * 4 + [_const_spec(a.shape) for a in post_in[4:]]
    out = pl.pallas_call(
        functools.partial(_post_kernel, S // POST_ROWS),
        grid=(T // (POST_SUB * POST_ROWS),),
        in_specs=post_specs,
        out_specs=ptok,
        out_shape=tok_shape,
        scratch_shapes=[pltpu.VMEM((8, D_FF), F32)],
        compiler_params=pltpu.CompilerParams(dimension_semantics=("arbitrary",),
                                             vmem_limit_bytes=VMEM_LIMIT),
        name="post",
    )(*post_in)
    return out.reshape(B, S, D)
```

```python
import functools
import math

import jax
import jax.numpy as jnp
from jax import lax
from jax.experimental import pallas as pl
from jax.experimental.pallas import tpu as pltpu

D_MODEL = 1024
SGU_BLOCK = 128
SGU_GROUPS = 8
HEAD = 64
HEADS = D_MODEL // HEAD
DECAY_LORA = 64
ICLR_LORA = 64
GATE_LORA = 160
GATE_LORA_PAD = 256
D_FF = 2688
CHUNK = 64
LN_EPS = 1e-5
GN_EPS = 64e-5
ALPHA = 2.0 ** 0.25
LANES = 128
LORA_PAIR = DECAY_LORA + ICLR_LORA
assert LORA_PAIR == LANES and SGU_BLOCK == LANES and 2 * HEAD == LANES
RK_COLS = 3 * D_MODEL + LORA_PAIR + GATE_LORA_PAD

PRE_ROWS = 256
POST_ROWS = 256
V7X_VMEM_BYTES = 64 * 1024 * 1024
VMEM_LIMIT = V7X_VMEM_BYTES * 7 // 8
SCAN_SUB = 4
SCAN_LAG = 3
POST_SUB = 2
POST_LAG = 2

F32 = jnp.float32
BF16 = jnp.bfloat16


def _ln(x, g, b, eps):
    mu = jnp.mean(x, -1, keepdims=True)
    xc = x - mu
    var = jnp.mean(xc * xc, -1, keepdims=True)
    return xc * lax.rsqrt(var + eps) * g + b


def _gelu(x):
    return 0.5 * x * (1.0 + lax.erf(x * (1.0 / math.sqrt(2.0))))


def _sigmoid(x):
    return 0.5 * jnp.tanh(0.5 * x) + 0.5


def _bdot(a, b):
    return jnp.dot(a.astype(BF16), b.astype(BF16), preferred_element_type=F32)


def _head_sum(q, ones_blk):
    qb = q.astype(BF16)
    cols = [jnp.dot(qb[:, g * LANES:(g + 1) * LANES], ones_blk, preferred_element_type=F32)
            for g in range(q.shape[1] // LANES)]
    return jnp.concatenate(cols, axis=1)


def _shift_rows(x, carry, k):
    rolled = pltpu.roll(x, k, 0)
    prev = pltpu.roll(carry, k, 0)
    row = lax.broadcasted_iota(jnp.int32, x.shape, 0)
    head = jnp.concatenate([prev, rolled[8:]], axis=0)
    return jnp.where(row < k, head, rolled)


def _pre_kernel(tiles_per_seq,
                x_ref, lng_ref, lnb_ref, wall_ref, bg_ref, mu_ref,
                sg_ref, sb_ref, ws_ref, bst_ref, w0_ref, w2_ref, a0_ref, a2_ref, g2_ref,
                kk_ref, ka_ref, ones_ref,
                r_out, lw_out, k_out, v_out, kk_out, a_out, ca_out, c1_out, hs_out,
                carry_ref):
    i = pl.program_id(0)

    @pl.when(i % tiles_per_seq == 0)
    def _():
        carry_ref[...] = jnp.zeros_like(carry_ref)

    h = _ln(x_ref[...], lng_ref[...], lnb_ref[...], LN_EPS)
    hb = h.astype(BF16)
    hs_out[...] = ALPHA * h

    proj = jnp.dot(hb, wall_ref[...], preferred_element_type=F32)
    uv = proj[:, :2 * D_MODEL]
    rk = proj[:, 2 * D_MODEL:2 * D_MODEL + RK_COLS]
    gates_pre = proj[:, 2 * D_MODEL + RK_COLS:]

    u = _gelu(uv[:, :D_MODEL])
    vn = _ln(_gelu(uv[:, D_MODEL:]), sg_ref[...], sb_ref[...], LN_EPS).astype(BF16)
    rr = lax.broadcasted_iota(jnp.int32, (SGU_BLOCK, SGU_BLOCK), 0) // CHUNK
    cc = lax.broadcasted_iota(jnp.int32, (SGU_BLOCK, SGU_BLOCK), 1) // CHUNK
    blk_mask = rr >= cc
    bst = bst_ref[...]
    z_rows = []
    for nb in range(PRE_ROWS // SGU_BLOCK):
        z_cols = []
        for g in range(SGU_GROUPS):
            wm = jnp.where(blk_mask, ws_ref[g], 0.0).astype(BF16)
            vb = vn[nb * SGU_BLOCK:(nb + 1) * SGU_BLOCK, g * LANES:(g + 1) * LANES]
            z_cols.append(jnp.dot(wm, vb, preferred_element_type=F32) + bst[:, g:g + 1])
        z_rows.append(jnp.concatenate(z_cols, axis=1))
    ya = u * jnp.concatenate(z_rows, axis=0)

    gates = _sigmoid(gates_pre + bg_ref[...])
    ca_out[...] = gates[:, :D_MODEL] * ya
    gb = gates[:, D_MODEL:]

    rk_prev = _shift_rows(rk, carry_ref[...], 1)
    carry_ref[...] = rk[PRE_ROWS - 8:, :]
    rk = rk + (rk_prev - rk) * mu_ref[...]
    r = rk[:, :D_MODEL]
    k = rk[:, D_MODEL:2 * D_MODEL]
    val = rk[:, 2 * D_MODEL:3 * D_MODEL]
    wdad = rk[:, 3 * D_MODEL:3 * D_MODEL + LORA_PAIR]
    gd = rk[:, 3 * D_MODEL + LORA_PAIR:]

    zw = w0_ref[...] + _bdot(jnp.tanh(wdad), w2_ref[...])
    lw_out[...] = (-math.exp(-0.5)) * _sigmoid(zw)
    a_lr = _sigmoid(a0_ref[...] + _bdot(wdad, a2_ref[...]))
    g = _bdot(_sigmoid(gd), g2_ref[...])
    c1_out[...] = gb * g

    kk = k * kk_ref[...]
    kk_out[...] = kk * lax.rsqrt(jnp.maximum(_head_sum(kk * kk, ones_ref[...]), 1e-24))
    k_out[...] = k * (1.0 + (a_lr - 1.0) * ka_ref[...])
    r_out[...] = r
    v_out[...] = val
    a_out[...] = a_lr


def _dot_nt(a, b):
    return lax.dot_general(a, b, (((1,), (1,)), ((), ())), preferred_element_type=F32)


def _split(x):
    hi = x.astype(BF16)
    return hi, (x - hi.astype(F32)).astype(BF16)


def _interleave(gens, lag):
    gens = list(gens)
    live = []
    step = 0
    while gens or live:
        if gens and step % lag == 0:
            live.insert(0, gens.pop(0))
        for g in list(live):
            if next(g, "done") == "done":
                live.remove(g)
        step += 1


def _scan_kernel(r_ref, lw_ref, k_ref, v_ref, kk_ref, a_ref, rk_ref, gng_ref, gnb_ref,
                 tri2_ref, ones2_ref, y_out, state_ref):
    c = pl.program_id(0)

    @pl.when(c == 0)
    def _():
        state_ref[...] = jnp.zeros_like(state_ref)

    L = CHUNK
    P = 2 * HEAD
    NB = r_ref.shape[0]
    n_sub = r_ref.shape[1] // L

    t_row = lax.broadcasted_iota(jnp.int32, (L, P), 0)
    lane = lax.broadcasted_iota(jnp.int32, (L, P), 1)
    first = lane < HEAD
    s_col = jnp.where(first, lane, lane - HEAD)
    strict = s_col < t_row
    incl = s_col <= t_row
    diag = s_col == t_row
    ones2 = ones2_ref[...]
    zero_b = jnp.zeros((L, P), BF16)
    pairs = range(NB * HEADS // 2)
    sls = [slice(p * P, (p + 1) * P) for p in pairs]

    def bd(x):
        return jnp.concatenate([jnp.where(first, x, 0), jnp.where(first, 0, x)], axis=0)

    def nat_t(x):
        t = bd(x).T
        return t[:L] + t[L:]

    def sums(tiles):
        s = jnp.dot(jnp.concatenate([t.astype(BF16) for t in tiles], axis=0), ones2[:P],
                    preferred_element_type=F32)
        return [s[i * L:(i + 1) * L] for i in range(len(tiles))]

    def chunk(j):
        rows = slice(j * L, (j + 1) * L)
        wide = lambda ref: jnp.concatenate([ref[b, rows, :] for b in range(NB)], axis=1)
        tiled = lambda ref: jnp.concatenate([ref[...]] * NB, axis=1)
        lw = wide(lw_ref)
        lw_hi, lw_lo = _split(lw)
        cs = jnp.dot(tri2_ref[...], jnp.concatenate([lw_hi, lw_lo], axis=0), preferred_element_type=F32)
        e_pos = jnp.exp(cs)
        e_neg = jnp.exp(-cs)
        p_end = e_pos[L - 1:L, :]
        yield
        r = wide(r_ref)
        kh = wide(k_ref)
        v = wide(v_ref)
        kk = wide(kk_ref)
        a_t = -kk * jnp.exp(cs - lw)
        r_t = r * e_pos
        b_t = kk * wide(a_ref) * e_neg
        k_t = kh * e_neg
        at_b = [a_t[:, s].astype(BF16) for s in sls]
        rt_b = [r_t[:, s].astype(BF16) for s in sls]
        bt_bd = [bd(b_t[:, s].astype(BF16)) for s in sls]
        kt_bd = [bd(k_t[:, s].astype(BF16)) for s in sls]
        vbd = [bd(v[:, s].astype(BF16)) for s in sls]
        yield
        b_h = b_t * p_end
        k_h = k_t * p_end
        bk_t = [jnp.concatenate([zero_b, nat_t(b_h[:, s]).astype(BF16), nat_t(k_h[:, s]).astype(BF16)], axis=1)
                for s in sls]
        bonus_w = r * kh * tiled(rk_ref)
        yield
        ar = [jnp.concatenate([at_b[p], rt_b[p]], axis=0) for p in pairs]
        gb = [_dot_nt(ar[p], bt_bd[p]) for p in pairs]
        gk = [_dot_nt(ar[p], kt_bd[p]) for p in pairs]
        a_ak = [jnp.where(strict, gk[p][:L], 0.0).astype(BF16) for p in pairs]
        a_rb = [jnp.where(incl, gb[p][L:], 0.0).astype(BF16) for p in pairs]
        a_rk = [jnp.where(incl, gk[p][L:], 0.0).astype(BF16) for p in pairs]
        n = [jnp.where(strict, gb[p][:L], 0.0) for p in pairs]
        x = [jnp.where(diag, 1.0, 0.0) + n[p] for p in pairs]
        yield
        n = [jnp.dot(n[p].astype(BF16), bd(n[p].astype(BF16)), preferred_element_type=F32) for p in pairs]
        yield
        for _ in range(4):
            o = [jnp.dot(jnp.concatenate([n[p], x[p]], axis=0).astype(BF16), bd(n[p].astype(BF16)),
                         preferred_element_type=F32) for p in pairs]
            n = [o[p][:L] for p in pairs]
            x = [x[p] + o[p][L:] for p in pairs]
            yield
        x = [(x[p] + jnp.dot(x[p].astype(BF16), bd(n[p].astype(BF16)), preferred_element_type=F32)
              ).astype(BF16) for p in pairs]
        yield
        yh_lhs = [jnp.concatenate([jnp.concatenate([rt_b[p], a_rb[p], a_rk[p]], axis=1), bk_t[p]], axis=0)
                  for p in pairs]
        bonus = sums([bonus_w[:, s] for s in sls])
        pe_hi = p_end.astype(BF16).astype(F32)
        pe_lo = p_end - pe_hi
        p_t = jnp.dot(jnp.concatenate(
            [jnp.concatenate([jnp.where(diag, pe_hi[:, s], 0.0), jnp.where(diag, pe_lo[:, s], 0.0)], axis=1)
             for s in sls], axis=0).astype(BF16), ones2, preferred_element_type=F32)
        yield
        hbd = [bd(state_ref[p].astype(BF16)) for p in pairs]
        u = [jnp.dot(jnp.concatenate([at_b[p], a_ak[p]], axis=1), jnp.concatenate([hbd[p], vbd[p]], axis=0),
                     preferred_element_type=F32) for p in pairs]
        yield
        u = [jnp.dot(x[p], bd(u[p].astype(BF16)), preferred_element_type=F32) for p in pairs]
        yield
        yh = [jnp.dot(yh_lhs[p], jnp.concatenate([hbd[p], bd(u[p].astype(BF16)), vbd[p]], axis=0),
                      preferred_element_type=F32) for p in pairs]
        for p in pairs:
            state_ref[p] = state_ref[p] * p_t[p * L:(p + 1) * L] + yh[p][L:]
        yield
        y = [yh[p][:L] for p in pairs]
        mu = sums(y)
        yc = [y[p] - mu[p] * (1.0 / HEAD) for p in pairs]
        yield
        var = sums([yc[p] * yc[p] for p in pairs])
        gn_g = tiled(gng_ref)
        gn_b = tiled(gnb_ref)
        for p in pairs:
            s = sls[p]
            b, hp = divmod(p, HEADS // 2)
            y_out[b, rows, hp * P:(hp + 1) * P] = (
                yc[p] * lax.rsqrt(var[p] * (1.0 / HEAD) + GN_EPS) * gn_g[:, s] + gn_b[:, s]
                + bonus[p] * v[:, s])

    assert SCAN_LAG >= 3
    _interleave([chunk(j) for j in range(n_sub)], SCAN_LAG)


def _post_kernel(tiles_per_seq,
                 hs_ref, ca_ref, c1_ref, yb_ref, wo_ref, l1g_ref, l1b_ref,
                 wup_ref, cw_ref, cb_ref, wdn_ref, l2g_ref, l2b_ref,
                 out_ref, carry_ref):
    i = pl.program_id(0)
    n_sub = hs_ref.shape[0] // POST_ROWS

    @pl.when(i % (tiles_per_seq // n_sub) == 0)
    def _():
        carry_ref[...] = jnp.zeros_like(carry_ref)

    def tile(t):
        rows = slice(t * POST_ROWS, (t + 1) * POST_ROWS)
        mix = (ca_ref[rows, :] + c1_ref[rows, :] * yb_ref[rows, :]).astype(BF16)
        yield
        t1 = jnp.dot(mix, wo_ref[...], preferred_element_type=F32)
        yield
        x1 = _ln(hs_ref[rows, :] + t1, l1g_ref[...], l1b_ref[...], LN_EPS)
        x1b = x1.astype(BF16)
        yield
        up = jnp.dot(x1b, wup_ref[...], preferred_element_type=F32)
        yield
        gate = up[:, :D_FF]
        val = up[:, D_FF:]
        carry = carry_ref[...]
        g1 = _shift_rows(gate, carry, 1)
        g2 = _shift_rows(gate, carry, 2)
        carry_ref[...] = gate[POST_ROWS - 8:, :]
        cw = cw_ref[...]
        conv = cb_ref[...] + g2 * cw[0:1, :] + g1 * cw[1:2, :] + gate * cw[2:3, :]
        act = (_gelu(conv) * val).astype(BF16)
        yield
        ffn = jnp.dot(act, wdn_ref[...], preferred_element_type=F32)
        yield
        out_ref[rows, :] = _ln(ALPHA * x1 + ffn, l2g_ref[...], l2b_ref[...], LN_EPS)

    _interleave([tile(t) for t in range(n_sub)], POST_LAG)


def _const_spec(shape):
    nd = len(shape)
    return pl.BlockSpec(shape, lambda *_: (0,) * nd, pipeline_mode=pl.Buffered(1))


def kernel(x, ln_in_g, ln_in_b, w_in, b_gate, mu_shift, sgu_ln_g, sgu_ln_b, w_s, b_s, w0, w2, a0, a2, g2, k_k, k_a, r_k, lnx_g, lnx_b, w_o, ln1_g, ln1_b, w_up, conv_w, conv_b, w_down, ln2_g, ln2_b):
    B, S, D = x.shape
    assert D == D_MODEL and w_in.shape[0] == 1
    T = B * S
    row = lambda p: p.reshape(1, -1).astype(F32)
    xf = x.reshape(T, D)

    wi = w_in[0]
    c_gates = 2 * D + 3 * D + LORA_PAIR + GATE_LORA
    w_all = jnp.concatenate([wi[:, :c_gates], jnp.zeros((D, GATE_LORA_PAD - GATE_LORA), F32), wi[:, c_gates:]],
                            axis=1).astype(BF16)
    mu =jnp.concatenate([mu_shift[0], jnp.zeros((GATE_LORA_PAD - GATE_LORA,), F32)]).reshape(1, -1)
    w2p = jnp.concatenate([w2[0], jnp.zeros((ICLR_LORA, D), F32)], axis=0).astype(BF16)
    a2p = jnp.concatenate([jnp.zeros((DECAY_LORA, D), F32), a2[0]], axis=0).astype(BF16)
    g2p = jnp.concatenate([g2[0], jnp.zeros((GATE_LORA_PAD - GATE_LORA, D), F32)], axis=0).astype(BF16)
    lane = jnp.arange(LANES) // HEAD
    ones_blk = (lane[:, None] == lane[None, :]).astype(BF16)

    tiles_per_seq = S // PRE_ROWS
    tok = pl.BlockSpec((PRE_ROWS, D), lambda i: (i, 0))
    pre_in = [xf, row(ln_in_g), row(ln_in_b), w_all, row(b_gate), mu,
              row(sgu_ln_g), row(sgu_ln_b), w_s[0], b_s[0].T, row(w0), w2p, row(a0), a2p, g2p,
              row(k_k), row(k_a), ones_blk]
    pre_specs = [tok] + [_const_spec(a.shape) for a in pre_in[1:]]
    tok_shape = jax.ShapeDtypeStruct((T, D), F32)
    r_t, lw_t, k_t, v_t, kk_t, a_t, ca_t, c1_t, hs_t = pl.pallas_call(
        functools.partial(_pre_kernel, tiles_per_seq),
        grid=(T // PRE_ROWS,),
        in_specs=pre_specs,
        out_specs=[tok] * 9,
        out_shape=[tok_shape] * 9,
        scratch_shapes=[pltpu.VMEM((8, RK_COLS), F32)],
        compiler_params=pltpu.CompilerParams(dimension_semantics=("arbitrary",),
                                             vmem_limit_bytes=VMEM_LIMIT),
        name="pre",
    )(*pre_in)

    n_chunks = S // CHUNK
    ctok = pl.BlockSpec((B, SCAN_SUB * CHUNK, D), lambda c: (0, c, 0))
    frame = jnp.arange(CHUNK)
    tri = (frame[:, None] >= frame[None, :]).astype(BF16)
    tri2 = jnp.concatenate([tri, tri], axis=1)
    ones2 = jnp.concatenate([ones_blk, ones_blk], axis=0)
    seq = lambda t: t.reshape(B, S, D)
    scan_in = [seq(r_t), seq(lw_t), seq(k_t), seq(v_t), seq(kk_t), seq(a_t),
               row(r_k), row(lnx_g), row(lnx_b), tri2, ones2]
    yb_t = pl.pallas_call(
        _scan_kernel,
        grid=(n_chunks // SCAN_SUB,),
        in_specs=[ctok] * 6 + [_const_spec(a.shape) for a in scan_in[6:]],
        out_specs=ctok,
        out_shape=jax.ShapeDtypeStruct((B, S, D), F32),
        scratch_shapes=[pltpu.VMEM((B * HEADS // 2, HEAD, 2 * HEAD), F32)],
        compiler_params=pltpu.CompilerParams(dimension_semantics=("arbitrary",),
                                             vmem_limit_bytes=VMEM_LIMIT),
        name="scan",
    )(*scan_in).reshape(T, D)

    ptok = pl.BlockSpec((POST_SUB * POST_ROWS, D), lambda i: (i, 0))
    post_in = [hs_t, ca_t, c1_t, yb_t, w_o[0].astype(BF16),
               row(ln1_g), row(ln1_b), w_up[0].astype(BF16), conv_w[0], row(conv_b),
               w_down[0].astype(BF16), row(ln2_g), row(ln2_b)]
    post_specs = [ptok] * 4 + [_const_spec(a.shape) for a in post_in[4:]]
    out = pl.pallas_call(
        functools.partial(_post_kernel, S // POST_ROWS),
        grid=(T // (POST_SUB * POST_ROWS),),
        in_specs=post_specs,
        out_specs=ptok,
        out_shape=tok_shape,
        scratch_shapes=[pltpu.VMEM((8, D_FF), F32)],
        compiler_params=pltpu.CompilerParams(dimension_semantics=("arbitrary",),
                                             vmem_limit_bytes=VMEM_LIMIT),
        name="post",
    )(*post_in)
    return out.reshape(B, S, D)
```

```python
import functools
import math

import jax
import jax.numpy as jnp
from jax import lax
from jax.experimental import pallas as pl
from jax.experimental.pallas import tpu as pltpu

D_MODEL = 1024
SGU_BLOCK = 128
SGU_GROUPS = 8
HEAD = 64
HEADS = D_MODEL // HEAD
DECAY_LORA = 64
ICLR_LORA = 64
GATE_LORA = 160
GATE_LORA_PAD = 256
D_FF = 2688
CHUNK = 64
LN_EPS = 1e-5
GN_EPS = 64e-5
ALPHA = 2.0 ** 0.25
LANES = 128
LORA_PAIR = DECAY_LORA + ICLR_LORA
assert LORA_PAIR == LANES and SGU_BLOCK == LANES and 2 * HEAD == LANES
RK_COLS = 3 * D_MODEL + LORA_PAIR + GATE_LORA_PAD

PRE_ROWS = 256
POST_ROWS = 256
V7X_VMEM_BYTES = 64 * 1024 * 1024
VMEM_LIMIT = V7X_VMEM_BYTES * 7 // 8
SCAN_SUB = 4
SCAN_LAG = 3
POST_SUB = 2
POST_LAG = 2

F32 = jnp.float32
BF16 = jnp.bfloat16


def _ln(x, g, b, eps):
    mu = jnp.mean(x, -1, keepdims=True)
    xc = x - mu
    var = jnp.mean(xc * xc, -1, keepdims=True)
    return xc * lax.rsqrt(var + eps) * g + b


def _gelu(x):
    return 0.5 * x * (1.0 + lax.erf(x * (1.0 / math.sqrt(2.0))))


def _sigmoid(x):
    return 0.5 * jnp.tanh(0.5 * x) + 0.5


def _bdot(a, b):
    return jnp.dot(a.astype(BF16), b.astype(BF16), preferred_element_type=F32)


def _head_sum(q, ones_blk):
    qb = q.astype(BF16)
    cols = [jnp.dot(qb[:, g * LANES:(g + 1) * LANES], ones_blk, preferred_element_type=F32)
            for g in range(q.shape[1] // LANES)]
    return jnp.concatenate(cols, axis=1)


def _shift_rows(x, carry, k):
    rolled = pltpu.roll(x, k, 0)
    prev = pltpu.roll(carry, k, 0)
    row = lax.broadcasted_iota(jnp.int32, x.shape, 0)
    head = jnp.concatenate([prev, rolled[8:]], axis=0)
    return jnp.where(row < k, head, rolled)


def _pre_kernel(tiles_per_seq,
                x_ref, lng_ref, lnb_ref, wall_ref, bg_ref, mu_ref,
                sg_ref, sb_ref, ws_ref, bst_ref, w0_ref, w2_ref, a0_ref, a2_ref, g2_ref,
                kk_ref, ka_ref, ones_ref,
                r_out, lw_out, k_out, v_out, kk_out, a_out, ca_out, c1_out, hs_out,
                carry_ref):
    i = pl.program_id(0)

    @pl.when(i % tiles_per_seq == 0)
    def _():
        carry_ref[...] = jnp.zeros_like(carry_ref)

    h = _ln(x_ref[...], lng_ref[...], lnb_ref[...], LN_EPS)
    hb = h.astype(BF16)
    hs_out[...] = ALPHA * h

    proj = jnp.dot(hb, wall_ref[...], preferred_element_type=F32)
    uv = proj[:, :2 * D_MODEL]
    rk = proj[:, 2 * D_MODEL:2 * D_MODEL + RK_COLS]
    gates_pre = proj[:, 2 * D_MODEL + RK_COLS:]

    u = _gelu(uv[:, :D_MODEL])
    vn = _ln(_gelu(uv[:, D_MODEL:]), sg_ref[...], sb_ref[...], LN_EPS).astype(BF16)
    rr = lax.broadcasted_iota(jnp.int32, (SGU_BLOCK, SGU_BLOCK), 0) // CHUNK
    cc = lax.broadcasted_iota(jnp.int32, (SGU_BLOCK, SGU_BLOCK), 1) // CHUNK
    blk_mask = rr >= cc
    bst = bst_ref[...]
    z_rows = []
    for nb in range(PRE_ROWS // SGU_BLOCK):
        z_cols = []
        for g in range(SGU_GROUPS):
            wm = jnp.where(blk_mask, ws_ref[g], 0.0).astype(BF16)
            vb = vn[nb * SGU_BLOCK:(nb + 1) * SGU_BLOCK, g * LANES:(g + 1) * LANES]
            z_cols.append(jnp.dot(wm, vb, preferred_element_type=F32) + bst[:, g:g + 1])
        z_rows.append(jnp.concatenate(z_cols, axis=1))
    ya = u * jnp.concatenate(z_rows, axis=0)

    gates = _sigmoid(gates_pre + bg_ref[...])
    ca_out[...] = gates[:, :D_MODEL] * ya
    gb = gates[:, D_MODEL:]

    rk_prev = _shift_rows(rk, carry_ref[...], 1)
    carry_ref[...] = rk[PRE_ROWS - 8:, :]
    rk = rk + (rk_prev - rk) * mu_ref[...]
    r = rk[:, :D_MODEL]
    k = rk[:, D_MODEL:2 * D_MODEL]
    val = rk[:, 2 * D_MODEL:3 * D_MODEL]
    wdad = rk[:, 3 * D_MODEL:3 * D_MODEL + LORA_PAIR]
    gd = rk[:, 3 * D_MODEL + LORA_PAIR:]

    zw = w0_ref[...] + _bdot(jnp.tanh(wdad), w2_ref[...])
    lw_out[...] = (-math.exp(-0.5)) * _sigmoid(zw)
    a_lr = _sigmoid(a0_ref[...] + _bdot(wdad, a2_ref[...]))
    g = _bdot(_sigmoid(gd), g2_ref[...])
    c1_out[...] = gb * g

    kk = k * kk_ref[...]
    kk_out[...] = kk * lax.rsqrt(jnp.maximum(_head_sum(kk * kk, ones_ref[...]), 1e-24))
    k_out[...] = k * (1.0 + (a_lr - 1.0) * ka_ref[...])
    r_out[...] = r
    v_out[...] = val
    a_out[...] = a_lr


def _dot_nt(a, b):
    return lax.dot_general(a, b, (((1,), (1,)), ((), ())), preferred_element_type=F32)


def _split(x):
    hi = x.astype(BF16)
    return hi, (x - hi.astype(F32)).astype(BF16)


def _interleave(gens, lag):
    gens = list(gens)
    live = []
    step = 0
    while gens or live:
        if gens and step % lag == 0:
            live.insert(0, gens.pop(0))
        for g in list(live):
            if next(g, "done") == "done":
                live.remove(g)
        step += 1


def _scan_kernel(r_ref, lw_ref, k_ref, v_ref, kk_ref, a_ref, rk_ref, gng_ref, gnb_ref,
                 tri2_ref, ones2_ref, y_out, state_ref):
    c = pl.program_id(0)

    @pl.when(c == 0)
    def _():
        state_ref[...] = jnp.zeros_like(state_ref)

    L = CHUNK
    P = 2 * HEAD
    NB = r_ref.shape[0]
    n_sub = r_ref.shape[1] // L

    t_row = lax.broadcasted_iota(jnp.int32, (L, P), 0)
    lane = lax.broadcasted_iota(jnp.int32, (L, P), 1)
    first = lane < HEAD
    s_col = jnp.where(first, lane, lane - HEAD)
    strict = s_col < t_row
    incl = s_col <= t_row
    diag = s_col == t_row
    ones2 = ones2_ref[...]
    zero_b = jnp.zeros((L, P), BF16)
    pairs = range(NB * HEADS // 2)
    sls = [slice(p * P, (p + 1) * P) for p in pairs]

    def bd(x):
        return jnp.concatenate([jnp.where(first, x, 0), jnp.where(first, 0, x)], axis=0)

    def nat_t(x):
        t = bd(x).T
        return t[:L] + t[L:]

    def sums(tiles):
        s = jnp.dot(jnp.concatenate([t.astype(BF16) for t in tiles], axis=0), ones2[:P],
                    preferred_element_type=F32)
        return [s[i * L:(i + 1) * L] for i in range(len(tiles))]

    def chunk(j):
        rows = slice(j * L, (j + 1) * L)
        wide = lambda ref: jnp.concatenate([ref[b, rows, :] for b in range(NB)], axis=1)
        tiled = lambda ref: jnp.concatenate([ref[...]] * NB, axis=1)
        lw = wide(lw_ref)
        lw_hi, lw_lo = _split(lw)
        cs = jnp.dot(tri2_ref[...], jnp.concatenate([lw_hi, lw_lo], axis=0), preferred_element_type=F32)
        e_pos = jnp.exp(cs)
        e_neg = jnp.exp(-cs)
        p_end = e_pos[L - 1:L, :]
        yield
        r = wide(r_ref)
        kh = wide(k_ref)
        v = wide(v_ref)
        kk = wide(kk_ref)
        a_t = -kk * jnp.exp(cs - lw)
        r_t = r * e_pos
        b_t = kk * wide(a_ref) * e_neg
        k_t = kh * e_neg
        at_b = [a_t[:, s].astype(BF16) for s in sls]
        rt_b = [r_t[:, s].astype(BF16) for s in sls]
        bt_bd = [bd(b_t[:, s].astype(BF16)) for s in sls]
        kt_bd = [bd(k_t[:, s].astype(BF16)) for s in sls]
        vbd = [bd(v[:, s].astype(BF16)) for s in sls]
        yield
        b_h = b_t * p_end
        k_h = k_t * p_end
        bk_t = [jnp.concatenate([zero_b, nat_t(b_h[:, s]).astype(BF16), nat_t(k_h[:, s]).astype(BF16)], axis=1)
                for s in sls]
        bonus_w = r * kh * tiled(rk_ref)
        yield
        ar = [jnp.concatenate([at_b[p], rt_b[p]], axis=0) for p in pairs]
        gb = [_dot_nt(ar[p], bt_bd[p]) for p in pairs]
        gk = [_dot_nt(ar[p], kt_bd[p]) for p in pairs]
        a_ak = [jnp.where(strict, gk[p][:L], 0.0).astype(BF16) for p in pairs]
        a_rb = [jnp.where(incl, gb[p][L:], 0.0).astype(BF16) for p in pairs]
        a_rk = [jnp.where(incl, gk[p][L:], 0.0).astype(BF16) for p in pairs]
        n = [jnp.where(strict, gb[p][:L], 0.0) for p in pairs]
        x = [jnp.where(diag, 1.0, 0.0) + n[p] for p in pairs]
        yield
        n = [jnp.dot(n[p].astype(BF16), bd(n[p].astype(BF16)), preferred_element_type=F32) for p in pairs]
        yield
        for _ in range(4):
            o = [jnp.dot(jnp.concatenate([n[p], x[p]], axis=0).astype(BF16), bd(n[p].astype(BF16)),
                         preferred_element_type=F32) for p in pairs]
            n = [o[p][:L] for p in pairs]
            x = [x[p] + o[p][L:] for p in pairs]
            yield
        x = [(x[p] + jnp.dot(x[p].astype(BF16), bd(n[p].astype(BF16)), preferred_element_type=F32)
              ).astype(BF16) for p in pairs]
        yield
        yh_lhs = [jnp.concatenate([jnp.concatenate([rt_b[p], a_rb[p], a_rk[p]], axis=1), bk_t[p]], axis=0)
                  for p in pairs]
        bonus = sums([bonus_w[:, s] for s in sls])
        pe_hi = p_end.astype(BF16).astype(F32)
        pe_lo = p_end - pe_hi
        p_t = jnp.dot(jnp.concatenate(
            [jnp.concatenate([jnp.where(diag, pe_hi[:, s], 0.0), jnp.where(diag, pe_lo[:, s], 0.0)], axis=1)
             for s in sls], axis=0).astype(BF16), ones2, preferred_element_type=F32)
        yield
        hbd = [bd(state_ref[p].astype(BF16)) for p in pairs]
        u = [jnp.dot(jnp.concatenate([at_b[p], a_ak[p]], axis=1), jnp.concatenate([hbd[p], vbd[p]], axis=0),
                     preferred_element_type=F32) for p in pairs]
        yield
        u = [jnp.dot(x[p], bd(u[p].astype(BF16)), preferred_element_type=F32) for p in pairs]
        yield
        yh = [jnp.dot(yh_lhs[p], jnp.concatenate([hbd[p], bd(u[p].astype(BF16)), vbd[p]], axis=0),
                      preferred_element_type=F32) for p in pairs]
        for p in pairs:
            state_ref[p] = state_ref[p] * p_t[p * L:(p + 1) * L] + yh[p][L:]
        yield
        y = [yh[p][:L] for p in pairs]
        mu = sums(y)
        yc = [y[p] - mu[p] * (1.0 / HEAD) for p in pairs]
        yield
        var = sums([yc[p] * yc[p] for p in pairs])
        gn_g = tiled(gng_ref)
        gn_b = tiled(gnb_ref)
        for p in pairs:
            s = sls[p]
            b, hp = divmod(p, HEADS // 2)
            y_out[b, rows, hp * P:(hp + 1) * P] = (
                yc[p] * lax.rsqrt(var[p] * (1.0 / HEAD) + GN_EPS) * gn_g[:, s] + gn_b[:, s]
                + bonus[p] * v[:, s])

    assert SCAN_LAG >= 3
    _interleave([chunk(j) for j in range(n_sub)], SCAN_LAG)


def _post_kernel(tiles_per_seq,
                 hs_ref, ca_ref, c1_ref, yb_ref, wo_ref, l1g_ref, l1b_ref,
                 wup_ref, cw_ref, cb_ref, wdn_ref, l2g_ref, l2b_ref,
                 out_ref, carry_ref):
    i = pl.program_id(0)
    n_sub = hs_ref.shape[0] // POST_ROWS

    @pl.when(i % (tiles_per_seq // n_sub) == 0)
    def _():
        carry_ref[...] = jnp.zeros_like(carry_ref)

    def tile(t):
        rows = slice(t * POST_ROWS, (t + 1) * POST_ROWS)
        mix = (ca_ref[rows, :] + c1_ref[rows, :] * yb_ref[rows, :]).astype(BF16)
        yield
        t1 = jnp.dot(mix, wo_ref[...], preferred_element_type=F32)
        yield
        x1 = _ln(hs_ref[rows, :] + t1, l1g_ref[...], l1b_ref[...], LN_EPS)
        x1b = x1.astype(BF16)
        yield
        up = jnp.dot(x1b, wup_ref[...], preferred_element_type=F32)
        yield
        gate = up[:, :D_FF]
        val = up[:, D_FF:]
        carry = carry_ref[...]
        g1 = _shift_rows(gate, carry, 1)
        g2 = _shift_rows(gate, carry, 2)
        carry_ref[...] = gate[POST_ROWS - 8:, :]
        cw = cw_ref[...]
        conv = cb_ref[...] + g2 * cw[0:1, :] + g1 * cw[1:2, :] + gate * cw[2:3, :]
        act = (_gelu(conv) * val).astype(BF16)
        yield
        ffn = jnp.dot(act, wdn_ref[...], preferred_element_type=F32)
        yield
        out_ref[rows, :] = _ln(ALPHA * x1 + ffn, l2g_ref[...], l2b_ref[...], LN_EPS)

    _interleave([tile(t) for t in range(n_sub)], POST_LAG)


def _const_spec(shape):
    nd = len(shape)
    return pl.BlockSpec(shape, lambda *_: (0,) * nd, pipeline_mode=pl.Buffered(1))


def kernel(x, ln_in_g, ln_in_b, w_in, b_gate, mu_shift, sgu_ln_g, sgu_ln_b, w_s, b_s, w0, w2, a0, a2, g2, k_k, k_a, r_k, lnx_g, lnx_b, w_o, ln1_g, ln1_b, w_up, conv_w, conv_b, w_down, ln2_g, ln2_b):
    B, S, D = x.shape
    assert D == D_MODEL and w_in.shape[0] == 1
    T = B * S
    row = lambda p: p.reshape(1, -1).astype(F32)
    xf = x.reshape(T, D)

    wi = w_in[0]
    c_gates = 2 * D + 3 * D + LORA_PAIR + GATE_LORA
    pad = GATE_LORA_PAD - GATE_LORA
    wb = wi.astype(BF16)
    w_all = jnp.pad(wb, ((0, 0), (0, pad)))
    w_all = w_all.at[:, c_gates + pad:].set(wb[:, c_gates:])
    w_all = w_all.at[:, c_gates:c_gates + pad].set(0)
    mu =jnp.concatenate([mu_shift[0], jnp.zeros((GATE_LORA_PAD - GATE_LORA,), F32)]).reshape(1, -1)
    w2p = jnp.concatenate([w2[0], jnp.zeros((ICLR_LORA, D), F32)], axis=0).astype(BF16)
    a2p = jnp.concatenate([jnp.zeros((DECAY_LORA, D), F32), a2[0]], axis=0).astype(BF16)
    g2p = jnp.concatenate([g2[0], jnp.zeros((GATE_LORA_PAD - GATE_LORA, D), F32)], axis=0).astype(BF16)
    lane = jnp.arange(LANES) // HEAD
    ones_blk = (lane[:, None] == lane[None, :]).astype(BF16)

    tiles_per_seq = S // PRE_ROWS
    tok = pl.BlockSpec((PRE_ROWS, D), lambda i: (i, 0))
    pre_in = [xf, row(ln_in_g), row(ln_in_b), w_all, row(b_gate), mu,
              row(sgu_ln_g), row(sgu_ln_b), w_s[0], b_s[0].T, row(w0), w2p, row(a0), a2p, g2p,
              row(k_k), row(k_a), ones_blk]
    pre_specs = [tok] + [_const_spec(a.shape) for a in pre_in[1:]]
    tok_shape = jax.ShapeDtypeStruct((T, D), F32)
    r_t, lw_t, k_t, v_t, kk_t, a_t, ca_t, c1_t, hs_t = pl.pallas_call(
        functools.partial(_pre_kernel, tiles_per_seq),
        grid=(T // PRE_ROWS,),
        in_specs=pre_specs,
        out_specs=[tok] * 9,
        out_shape=[tok_shape] * 9,
        scratch_shapes=[pltpu.VMEM((8, RK_COLS), F32)],
        compiler_params=pltpu.CompilerParams(dimension_semantics=("arbitrary",),
                                             vmem_limit_bytes=VMEM_LIMIT),
        name="pre",
    )(*pre_in)

    n_chunks = S // CHUNK
    ctok = pl.BlockSpec((B, SCAN_SUB * CHUNK, D), lambda c: (0, c, 0))
    frame = jnp.arange(CHUNK)
    tri = (frame[:, None] >= frame[None, :]).astype(BF16)
    tri2 = jnp.concatenate([tri, tri], axis=1)
    ones2 = jnp.concatenate([ones_blk, ones_blk], axis=0)
    seq = lambda t: t.reshape(B, S, D)
    scan_in = [seq(r_t), seq(lw_t), seq(k_t), seq(v_t), seq(kk_t), seq(a_t),
               row(r_k), row(lnx_g), row(lnx_b), tri2, ones2]
    yb_t = pl.pallas_call(
        _scan_kernel,
        grid=(n_chunks // SCAN_SUB,),
        in_specs=[ctok] * 6 + [_const_spec(a.shape) for a in scan_in[6:]],
        out_specs=ctok,
        out_shape=jax.ShapeDtypeStruct((B, S, D), F32),
        scratch_shapes=[pltpu.VMEM((B * HEADS // 2, HEAD, 2 * HEAD), F32)],
        compiler_params=pltpu.CompilerParams(dimension_semantics=("arbitrary",),
                                             vmem_limit_bytes=VMEM_LIMIT),
        name="scan",
    )(*scan_in).reshape(T, D)

    ptok = pl.BlockSpec((POST_SUB * POST_ROWS, D), lambda i: (i, 0))
    post_in = [hs_t, ca_t, c1_t, yb_t, w_o[0].astype(BF16),
               row(ln1_g), row(ln1_b), w_up[0].astype(BF16), conv_w[0], row(conv_b),
               w_down[0].astype(BF16), row(ln2_g), row(ln2_b)]
    post_specs = [ptok] * 4 + [_const_spec(a.shape) for a in post_in[4:]]
    out = pl.pallas_call(
        functools.partial(_post_kernel, S // POST_ROWS),
        grid=(T // (POST_SUB * POST_ROWS),),
        in_specs=post_specs,
        out_specs=ptok,
        out_shape=tok_shape,
        scratch_shapes=[pltpu.VMEM((8, D_FF), F32)],
        compiler_params=pltpu.CompilerParams(dimension_semantics=("arbitrary",),
                                             vmem_limit_bytes=VMEM_LIMIT),
        name="post",
    )(*post_in)
    return out.reshape(B, S, D)
```

```python
import functools
import math

import jax
import jax.numpy as jnp
from jax import lax
from jax.experimental import pallas as pl
from jax.experimental.pallas import tpu as pltpu

D_MODEL = 1024
SGU_BLOCK = 128
SGU_GROUPS = 8
HEAD = 64
HEADS = D_MODEL // HEAD
DECAY_LORA = 64
ICLR_LORA = 64
GATE_LORA = 160
GATE_LORA_PAD = 256
D_FF = 2688
CHUNK = 64
LN_EPS = 1e-5
GN_EPS = 64e-5
ALPHA = 2.0 ** 0.25
LANES = 128
LORA_PAIR = DECAY_LORA + ICLR_LORA
assert LORA_PAIR == LANES and SGU_BLOCK == LANES and 2 * HEAD == LANES
RK_COLS = 3 * D_MODEL + LORA_PAIR + GATE_LORA_PAD

PRE_ROWS = 256
POST_ROWS = 256
V7X_VMEM_BYTES = 64 * 1024 * 1024
VMEM_LIMIT = V7X_VMEM_BYTES * 7 // 8
SCAN_SUB = 4
SCAN_LAG = 3
POST_SUB = 2
POST_LAG = 2
PRE_SUB = 2
PRE_LAG = 2

F32 = jnp.float32
BF16 = jnp.bfloat16


def _ln(x, g, b, eps):
    mu = jnp.mean(x, -1, keepdims=True)
    xc = x - mu
    var = jnp.mean(xc * xc, -1, keepdims=True)
    return xc * lax.rsqrt(var + eps) * g + b


def _gelu(x):
    return 0.5 * x * (1.0 + lax.erf(x * (1.0 / math.sqrt(2.0))))


def _sigmoid(x):
    return 0.5 * jnp.tanh(0.5 * x) + 0.5


def _bdot(a, b):
    return jnp.dot(a.astype(BF16), b.astype(BF16), preferred_element_type=F32)


def _head_sum(q, ones_blk):
    qb = q.astype(BF16)
    cols = [jnp.dot(qb[:, g * LANES:(g + 1) * LANES], ones_blk, preferred_element_type=F32)
            for g in range(q.shape[1] // LANES)]
    return jnp.concatenate(cols, axis=1)


def _shift_rows(x, carry, k):
    rolled = pltpu.roll(x, k, 0)
    prev = pltpu.roll(carry, k, 0)
    row = lax.broadcasted_iota(jnp.int32, x.shape, 0)
    head = jnp.concatenate([prev, rolled[8:]], axis=0)
    return jnp.where(row < k, head, rolled)


def _interleave(gens, lag):
    gens = list(gens)
    live = []
    step = 0
    while gens or live:
        if gens and step % lag == 0:
            live.insert(0, gens.pop(0))
        for g in list(live):
            if next(g, "done") == "done":
                live.remove(g)
        step += 1


def _pre_kernel(x_ref, lng_ref, lnb_ref, wuv_ref, wg_ref, bg_ref, sg_ref, sb_ref, ws_ref, bst_ref,
                hb_out, hs_out, ca_out, gb_out):
    n_sub = x_ref.shape[0] // PRE_ROWS
    rr = lax.broadcasted_iota(jnp.int32, (SGU_BLOCK, SGU_BLOCK), 0) // CHUNK
    cc = lax.broadcasted_iota(jnp.int32, (SGU_BLOCK, SGU_BLOCK), 1) // CHUNK
    blk_mask = rr >= cc

    def tile(t):
        rows = slice(t * PRE_ROWS, (t + 1) * PRE_ROWS)
        h = _ln(x_ref[rows, :], lng_ref[...], lnb_ref[...], LN_EPS)
        hb = h.astype(BF16)
        hb_out[rows, :] = hb
        hs_out[rows, :] = ALPHA * h
        yield
        uv = jnp.dot(hb, wuv_ref[...], preferred_element_type=F32)
        yield
        gates_pre = jnp.dot(hb, wg_ref[...], preferred_element_type=F32)
        yield
        u = _gelu(uv[:, :D_MODEL])
        vn = _ln(_gelu(uv[:, D_MODEL:]), sg_ref[...], sb_ref[...], LN_EPS).astype(BF16)
        bst = bst_ref[...]
        z_rows = []
        for nb in range(PRE_ROWS // SGU_BLOCK):
            z_cols = []
            for grp in range(SGU_GROUPS):
                wm = jnp.where(blk_mask, ws_ref[grp], 0.0).astype(BF16)
                vb = vn[nb * SGU_BLOCK:(nb + 1) * SGU_BLOCK, grp * LANES:(grp + 1) * LANES]
                z_cols.append(jnp.dot(wm, vb, preferred_element_type=F32) + bst[:, grp:grp + 1])
            z_rows.append(jnp.concatenate(z_cols, axis=1))
        ya = u * jnp.concatenate(z_rows, axis=0)
        yield
        gates = _sigmoid(gates_pre + bg_ref[...])
        ca_out[rows, :] = gates[:, :D_MODEL] * ya
        gb_out[rows, :] = gates[:, D_MODEL:]

    _interleave([tile(t) for t in range(n_sub)], PRE_LAG)


def _dot_nt(a, b):
    return lax.dot_general(a, b, (((1,), (1,)), ((), ())), preferred_element_type=F32)


def _split(x):
    hi = x.astype(BF16)
    return hi, (x - hi.astype(F32)).astype(BF16)


def _scan_kernel(hb_ref, gb_ref, wrk_ref, mu_ref, w0_ref, w2_ref, a0_ref, a2_ref, g2_ref,
                 kkp_ref, kap_ref, ones_ref, rk_ref, gng_ref, gnb_ref, tri2_ref, ones2_ref,
                 y_out, c1_out, state_ref, carry_ref, tok_ref):
    c = pl.program_id(0)

    @pl.when(c == 0)
    def _():
        state_ref[...] = jnp.zeros_like(state_ref)
        carry_ref[...] = jnp.zeros_like(carry_ref)

    L = CHUNK
    P = 2 * HEAD
    NB = hb_ref.shape[0]
    n_sub = hb_ref.shape[1] // L
    r_ref, lw_ref, k_ref, v_ref, kk_ref, a_ref = (tok_ref.at[q] for q in range(6))

    rks = [jnp.dot(hb_ref[b], wrk_ref[...], preferred_element_type=F32) for b in range(NB)]
    for b in range(NB):
        rk = rks[b]
        rk_prev = _shift_rows(rk, carry_ref[b], 1)
        carry_ref[b] = rk[rk.shape[0] - 8:, :]
        rk = rk + (rk_prev - rk) * mu_ref[...]
        k = rk[:, D_MODEL:2 * D_MODEL]
        wdad = rk[:, 3 * D_MODEL:3 * D_MODEL + LORA_PAIR]
        gd = rk[:, 3 * D_MODEL + LORA_PAIR:]
        r_ref[b] = rk[:, :D_MODEL]
        v_ref[b] = rk[:, 2 * D_MODEL:3 * D_MODEL]
        zw = w0_ref[...] + _bdot(jnp.tanh(wdad), w2_ref[...])
        lw_ref[b] = (-math.exp(-0.5)) * _sigmoid(zw)
        a_lr = _sigmoid(a0_ref[...] + _bdot(wdad, a2_ref[...]))
        c1_out[b] = gb_ref[b] * _bdot(_sigmoid(gd), g2_ref[...])
        a_ref[b] = a_lr
        kk = k * kkp_ref[...]
        kk_ref[b] = kk * lax.rsqrt(jnp.maximum(_head_sum(kk * kk, ones_ref[...]), 1e-24))
        k_ref[b] = k * (1.0 + (a_lr - 1.0) * kap_ref[...])

    t_row = lax.broadcasted_iota(jnp.int32, (L, P), 0)
    lane = lax.broadcasted_iota(jnp.int32, (L, P), 1)
    first = lane < HEAD
    s_col = jnp.where(first, lane, lane - HEAD)
    strict = s_col < t_row
    incl = s_col <= t_row
    diag = s_col == t_row
    ones2 = ones2_ref[...]
    zero_b = jnp.zeros((L, P), BF16)
    pairs = range(NB * HEADS // 2)
    sls = [slice(p * P, (p + 1) * P) for p in pairs]

    def bd(x):
        return jnp.concatenate([jnp.where(first, x, 0), jnp.where(first, 0, x)], axis=0)

    def nat_t(x):
        t = bd(x).T
        return t[:L] + t[L:]

    def sums(tiles):
        s = jnp.dot(jnp.concatenate([t.astype(BF16) for t in tiles], axis=0), ones2[:P],
                    preferred_element_type=F32)
        return [s[i * L:(i + 1) * L] for i in range(len(tiles))]

    def chunk(j):
        rows = slice(j * L, (j + 1) * L)
        wide = lambda ref: jnp.concatenate([ref[b, rows, :] for b in range(NB)], axis=1)
        tiled = lambda ref: jnp.concatenate([ref[...]] * NB, axis=1)
        lw = wide(lw_ref)
        lw_hi, lw_lo = _split(lw)
        cs = jnp.dot(tri2_ref[...], jnp.concatenate([lw_hi, lw_lo], axis=0), preferred_element_type=F32)
        e_pos = jnp.exp(cs)
        e_neg = jnp.exp(-cs)
        p_end = e_pos[L - 1:L, :]
        yield
        r = wide(r_ref)
        kh = wide(k_ref)
        v = wide(v_ref)
        kk = wide(kk_ref)
        a_t = -kk * jnp.exp(cs - lw)
        r_t = r * e_pos
        b_t = kk * wide(a_ref) * e_neg
        k_t = kh * e_neg
        at_b = [a_t[:, s].astype(BF16) for s in sls]
        rt_b = [r_t[:, s].astype(BF16) for s in sls]
        bt_bd = [bd(b_t[:, s].astype(BF16)) for s in sls]
        kt_bd = [bd(k_t[:, s].astype(BF16)) for s in sls]
        vbd = [bd(v[:, s].astype(BF16)) for s in sls]
        yield
        b_h = b_t * p_end
        k_h = k_t * p_end
        bk_t = [jnp.concatenate([zero_b, nat_t(b_h[:, s]).astype(BF16), nat_t(k_h[:, s]).astype(BF16)], axis=1)
                for s in sls]
        bonus_w = r * kh * tiled(rk_ref)
        yield
        ar = [jnp.concatenate([at_b[p], rt_b[p]], axis=0) for p in pairs]
        gb = [_dot_nt(ar[p], bt_bd[p]) for p in pairs]
        gk = [_dot_nt(ar[p], kt_bd[p]) for p in pairs]
        a_ak = [jnp.where(strict, gk[p][:L], 0.0).astype(BF16) for p in pairs]
        a_rb = [jnp.where(incl, gb[p][L:], 0.0).astype(BF16) for p in pairs]
        a_rk = [jnp.where(incl, gk[p][L:], 0.0).astype(BF16) for p in pairs]
        n = [jnp.where(strict, gb[p][:L], 0.0) for p in pairs]
        x = [jnp.where(diag, 1.0, 0.0) + n[p] for p in pairs]
        yield
        n = [jnp.dot(n[p].astype(BF16), bd(n[p].astype(BF16)), preferred_element_type=F32) for p in pairs]
        yield
        for _ in range(4):
            o = [jnp.dot(jnp.concatenate([n[p], x[p]], axis=0).astype(BF16), bd(n[p].astype(BF16)),
                         preferred_element_type=F32) for p in pairs]
            n = [o[p][:L] for p in pairs]
            x = [x[p] + o[p][L:] for p in pairs]
            yield
        x = [(x[p] + jnp.dot(x[p].astype(BF16), bd(n[p].astype(BF16)), preferred_element_type=F32)
              ).astype(BF16) for p in pairs]
        yield
        yh_lhs = [jnp.concatenate([jnp.concatenate([rt_b[p], a_rb[p], a_rk[p]], axis=1), bk_t[p]], axis=0)
                  for p in pairs]
        bonus = sums([bonus_w[:, s] for s in sls])
        pe_hi = p_end.astype(BF16).astype(F32)
        pe_lo = p_end - pe_hi
        p_t = jnp.dot(jnp.concatenate(
            [jnp.concatenate([jnp.where(diag, pe_hi[:, s], 0.0), jnp.where(diag, pe_lo[:, s], 0.0)], axis=1)
             for s in sls], axis=0).astype(BF16), ones2, preferred_element_type=F32)
        yield
        hbd = [bd(state_ref[p].astype(BF16)) for p in pairs]
        u = [jnp.dot(jnp.concatenate([at_b[p], a_ak[p]], axis=1), jnp.concatenate([hbd[p], vbd[p]], axis=0),
                     preferred_element_type=F32) for p in pairs]
        yield
        u = [jnp.dot(x[p], bd(u[p].astype(BF16)), preferred_element_type=F32) for p in pairs]
        yield
        yh = [jnp.dot(yh_lhs[p], jnp.concatenate([hbd[p], bd(u[p].astype(BF16)), vbd[p]], axis=0),
                      preferred_element_type=F32) for p in pairs]
        for p in pairs:
            state_ref[p] = state_ref[p] * p_t[p * L:(p + 1) * L] + yh[p][L:]
        yield
        y = [yh[p][:L] for p in pairs]
        mu = sums(y)
        yc = [y[p] - mu[p] * (1.0 / HEAD) for p in pairs]
        yield
        var = sums([yc[p] * yc[p] for p in pairs])
        gn_g = tiled(gng_ref)
        gn_b = tiled(gnb_ref)
        for p in pairs:
            s = sls[p]
            b, hp = divmod(p, HEADS // 2)
            y_out[b, rows, hp * P:(hp + 1) * P] = (
                yc[p] * lax.rsqrt(var[p] * (1.0 / HEAD) + GN_EPS) * gn_g[:, s] + gn_b[:, s]
                + bonus[p] * v[:, s])

    assert SCAN_LAG >= 3
    _interleave([chunk(j) for j in range(n_sub)], SCAN_LAG)


def _post_kernel(tiles_per_seq,
                 hs_ref, ca_ref, c1_ref, yb_ref, wo_ref, l1g_ref, l1b_ref,
                 wup_ref, cw_ref, cb_ref, wdn_ref, l2g_ref, l2b_ref,
                 out_ref, carry_ref):
    i = pl.program_id(0)
    n_sub = hs_ref.shape[0] // POST_ROWS

    @pl.when(i % (tiles_per_seq // n_sub) == 0)
    def _():
        carry_ref[...] = jnp.zeros_like(carry_ref)

    def tile(t):
        rows = slice(t * POST_ROWS, (t + 1) * POST_ROWS)
        mix = (ca_ref[rows, :] + c1_ref[rows, :] * yb_ref[rows, :]).astype(BF16)
        yield
        t1 = jnp.dot(mix, wo_ref[...], preferred_element_type=F32)
        yield
        x1 = _ln(hs_ref[rows, :] + t1, l1g_ref[...], l1b_ref[...], LN_EPS)
        x1b = x1.astype(BF16)
        yield
        up = jnp.dot(x1b, wup_ref[...], preferred_element_type=F32)
        yield
        gate = up[:, :D_FF]
        val = up[:, D_FF:]
        carry = carry_ref[...]
        g1 = _shift_rows(gate, carry, 1)
        g2 = _shift_rows(gate, carry, 2)
        carry_ref[...] = gate[POST_ROWS - 8:, :]
        cw = cw_ref[...]
        conv = cb_ref[...] + g2 * cw[0:1, :] + g1 * cw[1:2, :] + gate * cw[2:3, :]
        act = (_gelu(conv) * val).astype(BF16)
        yield
        ffn = jnp.dot(act, wdn_ref[...], preferred_element_type=F32)
        yield
        out_ref[rows, :] = _ln(ALPHA * x1 + ffn, l2g_ref[...], l2b_ref[...], LN_EPS)

    _interleave([tile(t) for t in range(n_sub)], POST_LAG)


def _const_spec(shape):
    nd = len(shape)
    return pl.BlockSpec(shape, lambda *_: (0,) * nd, pipeline_mode=pl.Buffered(1))


def kernel(x, ln_in_g, ln_in_b, w_in, b_gate, mu_shift, sgu_ln_g, sgu_ln_b, w_s, b_s, w0, w2, a0, a2, g2, k_k, k_a, r_k, lnx_g, lnx_b, w_o, ln1_g, ln1_b, w_up, conv_w, conv_b, w_down, ln2_g, ln2_b):
    B, S, D = x.shape
    assert D == D_MODEL and w_in.shape[0] == 1
    T = B * S
    row = lambda p: p.reshape(1, -1).astype(F32)
    xf = x.reshape(T, D)

    wi = w_in[0]
    c_rk = 2 * D
    c_gates = c_rk + 3 * D + LORA_PAIR + GATE_LORA
    pad = GATE_LORA_PAD - GATE_LORA
    w_uv = wi[:, :c_rk].astype(BF16)
    w_rk = jnp.pad(wi[:, c_rk:c_gates].astype(BF16), ((0, 0), (0, pad)))
    w_g = wi[:, c_gates:].astype(BF16)
    mu = jnp.concatenate([mu_shift[0], jnp.zeros((pad,), F32)]).reshape(1, -1)
    w2p = jnp.concatenate([w2[0], jnp.zeros((ICLR_LORA, D), F32)], axis=0).astype(BF16)
    a2p = jnp.concatenate([jnp.zeros((DECAY_LORA, D), F32), a2[0]], axis=0).astype(BF16)
    g2p = jnp.concatenate([g2[0], jnp.zeros((pad, D), F32)], axis=0).astype(BF16)
    lane = jnp.arange(LANES) // HEAD
    ones_blk = (lane[:, None] == lane[None, :]).astype(BF16)

    tok = pl.BlockSpec((PRE_SUB * PRE_ROWS, D), lambda i: (i, 0))
    tok_shape = jax.ShapeDtypeStruct((T, D), F32)
    pre_in = [xf, row(ln_in_g), row(ln_in_b), w_uv, w_g, row(b_gate),
              row(sgu_ln_g), row(sgu_ln_b), w_s[0], b_s[0].T]
    hb_t, hs_t, ca_t, gb_t = pl.pallas_call(
        _pre_kernel,
        grid=(T // (PRE_SUB * PRE_ROWS),),
        in_specs=[tok] + [_const_spec(a.shape) for a in pre_in[1:]],
        out_specs=[tok] * 4,
        out_shape=[jax.ShapeDtypeStruct((T, D), BF16)] + [tok_shape] * 3,
        compiler_params=pltpu.CompilerParams(dimension_semantics=("arbitrary",),
                                             vmem_limit_bytes=VMEM_LIMIT),
        name="pre",
    )(*pre_in)

    n_chunks = S // CHUNK
    frames = SCAN_SUB * CHUNK
    ctok = pl.BlockSpec((B, frames, D), lambda c: (0, c, 0))
    frame = jnp.arange(CHUNK)
    tri = (frame[:, None] >= frame[None, :]).astype(BF16)
    tri2 = jnp.concatenate([tri, tri], axis=1)
    ones2 = jnp.concatenate([ones_blk, ones_blk], axis=0)
    seq = lambda t: t.reshape(B, S, D)
    scan_in = [seq(hb_t), seq(gb_t), w_rk, mu, row(w0), w2p, row(a0), a2p, g2p, row(k_k), row(k_a), ones_blk,
               row(r_k), row(lnx_g), row(lnx_b), tri2, ones2]
    seq_shape = jax.ShapeDtypeStruct((B, S, D), F32)
    yb_t, c1_t = pl.pallas_call(
        _scan_kernel,
        grid=(n_chunks // SCAN_SUB,),
        in_specs=[ctok] * 2 + [_const_spec(a.shape) for a in scan_in[2:]],
        out_specs=[ctok] * 2,
        out_shape=[seq_shape] * 2,
        scratch_shapes=[pltpu.VMEM((B * HEADS // 2, HEAD, 2 * HEAD), F32),
                        pltpu.VMEM((B, 8, RK_COLS), F32),
                        pltpu.VMEM((6, B, frames, D), F32)],
        compiler_params=pltpu.CompilerParams(dimension_semantics=("arbitrary",),
                                             vmem_limit_bytes=VMEM_LIMIT),
        name="scan",
    )(*scan_in)
    yb_t = yb_t.reshape(T, D)
    c1_t = c1_t.reshape(T, D)

    ptok = pl.BlockSpec((POST_SUB * POST_ROWS, D), lambda i: (i, 0))
    post_in = [hs_t, ca_t, c1_t, yb_t, w_o[0].astype(BF16),
               row(ln1_g), row(ln1_b), w_up[0].astype(BF16), conv_w[0], row(conv_b),
               w_down[0].astype(BF16), row(ln2_g), row(ln2_b)]
    post_specs = [ptok] * 4 + [_const_spec(a.shape) for a in post_in[4:]]
    out = pl.pallas_call(
        functools.partial(_post_kernel, S // POST_ROWS),
        grid=(T // (POST_SUB * POST_ROWS),),
        in_specs=post_specs,
        out_specs=ptok,
        out_shape=tok_shape,
        scratch_shapes=[pltpu.VMEM((8, D_FF), F32)],
        compiler_params=pltpu.CompilerParams(dimension_semantics=("arbitrary",),
                                             vmem_limit_bytes=VMEM_LIMIT),
        name="post",
    )(*post_in)
    return out.reshape(B, S, D)
```
